```python
import math
import jax, jax.numpy as jnp
from jax import lax
import numpy as np

D_MODEL = 1024
BATCH = 8
SEQ = 2048
DEPTH = 4

N_MEM = 256
EPS = 1e-6
DA_HEADS = 8
DA_HEAD_DIM = 64
DA_V_DIM = 2 * DA_HEAD_DIM
DA_WIDTH = DA_HEADS * DA_V_DIM
DA_QK_COLS = DA_HEADS * 2 * DA_HEAD_DIM
DA_LAMBDA_STD = 0.1
Q_BLOCK = 128
SSM_EXPAND = 2
SSM_INNER = SSM_EXPAND * D_MODEL
SSM_HEAD_DIM = 64
SSM_HEADS = SSM_INNER // SSM_HEAD_DIM
SSM_GROUPS = 4
SSM_STATE = 128
SSM_CONV = 5
SSM_CHUNK = 128
SSM_BC_COLS = SSM_GROUPS * SSM_STATE
SSM_CONV_CH = SSM_INNER + 2 * SSM_BC_COLS
SSM_DT_COLS = 2 * SSM_HEADS
DT_MIN = 0.001
DT_MAX = 0.1
XA_HEADS = 4
XA_HEAD_DIM = D_MODEL // XA_HEADS
XA_WIDTH = XA_HEADS * XA_HEAD_DIM
N_BRANCH = 3
GATE_COLS = N_BRANCH * D_MODEL
IN_SIZES = (DA_QK_COLS, DA_QK_COLS, DA_WIDTH, SSM_INNER, SSM_CONV_CH, SSM_DT_COLS, XA_WIDTH, GATE_COLS)
IN_COLS = sum(IN_SIZES)
BRANCH_IN = DA_WIDTH + SSM_INNER + XA_WIDTH
FFN_HIDDEN = -(-8 * D_MODEL // (3 * 256)) * 256

kernel_name = "hybrid_diffattn_bimamba2_memxattn_encoder"


def rms_norm(x, g):
    xf = x.astype(jnp.float32)
    y = xf * lax.rsqrt(jnp.mean(xf * xf, axis=-1, keepdims=True) + EPS)
    return (y * g.astype(jnp.float32)).astype(x.dtype)


def split_cols(t, sizes):
    out, off = [], 0
    for s in sizes:
        out.append(t[..., off:off + s])
        off += s
    return out


def alibi_slopes(n_heads):
    start = 2.0 ** (-8.0 / n_heads)
    return np.array([start ** (i + 1) for i in range(n_heads)], dtype=np.float32)


def diff_attention(q, k, v, lam, lambda_init, sub_g):
    b, s_len, h, _, dh = q.shape
    nblk = s_len // Q_BLOCK
    slopes = jnp.asarray(alibi_slopes(h))
    kpos = jnp.arange(s_len)
    qb = jnp.moveaxis((q * (dh ** -0.5)).reshape(b, nblk, Q_BLOCK, h, 2, dh), 1, 0)
    starts = jnp.arange(nblk) * Q_BLOCK

    def block(args):
        qi, start = args
        sc = jnp.einsum('bqhcd,bkhcd->bhcqk', qi, k, preferred_element_type=jnp.float32)
        qpos = start + jnp.arange(Q_BLOCK)
        dist = jnp.abs(qpos[:, None] - kpos[None, :]).astype(jnp.float32)
        sc = sc - slopes[None, :, None, None, None] * dist
        p = jax.nn.softmax(sc, axis=-1)
        attn = p[:, :, 0] - lam * p[:, :, 1]
        return jnp.einsum('bhqk,bkhe->bqhe', attn.astype(v.dtype), v)

    o = lax.map(block, (qb, starts))
    o = jnp.moveaxis(o, 0, 1).reshape(b, s_len, h, 2 * dh)
    o = rms_norm(o, sub_g) * (1.0 - lambda_init)
    return o.reshape(b, s_len, h * 2 * dh)


def segsum_exp(a):
    n = a.shape[-1]
    cs = jnp.cumsum(a, axis=-1)
    diff = cs[..., :, None] - cs[..., None, :]
    mask = jnp.tril(jnp.ones((n, n), dtype=bool))
    return jnp.exp(jnp.where(mask, diff, -jnp.inf))


def ssd_scan(x, dt, A, Bm, Cm):
    f32 = jnp.float32
    b, l, h, p = x.shape
    g, n = Bm.shape[-2:]
    r = h // g
    c = l // SSM_CHUNK
    q_len = SSM_CHUNK
    dtf = dt.astype(f32)
    xdt = (x.astype(f32) * dtf[..., None]).reshape(b, c, q_len, g, r, p)
    a = (dtf * A.astype(f32)).reshape(b, c, q_len, g, r).transpose(0, 1, 3, 4, 2)
    Bc = Bm.astype(f32).reshape(b, c, q_len, g, n)
    Cc = Cm.astype(f32).reshape(b, c, q_len, g, n)
    a_cs = jnp.cumsum(a, axis=-1)
    cb = jnp.einsum('bclgn,bcsgn->bcgls', Cc, Bc)
    m = cb[:, :, :, None] * segsum_exp(a)
    y_diag = jnp.einsum('bcgrls,bcsgrp->bclgrp', m, xdt)
    decay_states = jnp.exp(a_cs[..., -1:] - a_cs).transpose(0, 1, 4, 2, 3)
    states = jnp.einsum('bclgn,bclgrp->bcgrpn', Bc, xdt * decay_states[..., None])
    chunk_decay = jnp.exp(a_cs[..., -1])

    def step(carry, inp):
        st, dec = inp
        return carry * dec[..., None, None] + st, carry

    init = jnp.zeros((b, g, r, p, n), f32)
    _, prev = lax.scan(step, init, (jnp.moveaxis(states, 1, 0), jnp.moveaxis(chunk_decay, 1, 0)))
    prev = jnp.moveaxis(prev, 0, 1)
    decay_out = jnp.exp(a_cs).transpose(0, 1, 4, 2, 3)
    y_off = jnp.einsum('bclgn,bcgrpn->bclgrp', Cc, prev) * decay_out[..., None]
    return (y_diag + y_off).reshape(b, l, h, p).astype(x.dtype)


def mamba2_bidir(z, xbc, dt_raw, conv_w, conv_b, dt_bias, A_log, D_skip, norm_g):
    b, l, ch = xbc.shape
    pad = SSM_CONV // 2
    xbc = lax.conv_general_dilated(xbc, conv_w[:, None, :], window_strides=(1,),
                                   padding=[(pad, pad)], dimension_numbers=('NWC', 'WIO', 'NWC'),
                                   feature_group_count=ch) + conv_b
    xbc = jax.nn.silu(xbc)
    xs, Bm, Cm = split_cols(xbc, (SSM_INNER, SSM_BC_COLS, SSM_BC_COLS))
    xs = xs.reshape(b, l, SSM_HEADS, SSM_HEAD_DIM)
    Bm = Bm.reshape(b, l, SSM_GROUPS, SSM_STATE)
    Cm = Cm.reshape(b, l, SSM_GROUPS, SSM_STATE)
    dt = jax.nn.softplus(dt_raw.reshape(b, l, 2, SSM_HEADS).astype(jnp.float32)
                         + dt_bias.astype(jnp.float32))
    A = -jnp.exp(A_log.astype(jnp.float32))
    y_f = ssd_scan(xs, dt[:, :, 0], A[0], Bm, Cm)
    fl = lambda t: jnp.flip(t, axis=1)
    y_b = fl(ssd_scan(fl(xs), fl(dt[:, :, 1]), A[1], fl(Bm), fl(Cm)))
    y = y_f + y_b + xs * D_skip[:, None]
    y = y.reshape(b, l, SSM_INNER) * jax.nn.silu(z)
    y = rms_norm(y.reshape(b, l, SSM_GROUPS, SSM_INNER // SSM_GROUPS),
                 norm_g.reshape(SSM_GROUPS, SSM_INNER // SSM_GROUPS))
    return y.reshape(b, l, SSM_INNER)


def mem_cross_attention(q, mem_n, w_kv):
    b, m, _ = mem_n.shape
    k, v = split_cols(mem_n @ w_kv, (XA_WIDTH, XA_WIDTH))
    k = k.reshape(b, m, XA_HEADS, XA_HEAD_DIM)
    v = v.reshape(b, m, XA_HEADS, XA_HEAD_DIM)
    sc = jnp.einsum('bqhd,bkhd->bhqk', q * (XA_HEAD_DIM ** -0.5), k, preferred_element_type=jnp.float32)
    p = jax.nn.softmax(sc, axis=-1)
    o = jnp.einsum('bhqk,bkhd->bqhd', p.astype(v.dtype), v)
    return o.reshape(b, q.shape[1], XA_WIDTH)


def setup_inputs(seed: int = 0) -> dict:
    key = jax.random.key(seed)
    ks = jax.random.split(key, 24)
    f32 = jnp.float32

    def nrm(k, shape, scale):
        return jax.random.normal(k, shape, f32) * scale

    def gain(k, shape):
        return 1.0 + 0.02 * jax.random.normal(k, shape, f32)

    res_scale = (2.0 * DEPTH) ** -0.5
    dt0 = jnp.exp(jax.random.uniform(ks[10], (DEPTH, 2, SSM_HEADS), f32, math.log(DT_MIN), math.log(DT_MAX)))
    dt_bias = dt0 + jnp.log(-jnp.expm1(-dt0))
    A_log = jnp.log(jax.random.uniform(ks[11], (DEPTH, 2, SSM_HEADS), f32, 1.0, 16.0))
    return {
        "x": nrm(ks[0], (BATCH, SEQ, D_MODEL), 1.0),
        "mem": nrm(ks[1], (BATCH, N_MEM, D_MODEL), 1.0),
        "mix_norm_g": gain(ks[2], (DEPTH, D_MODEL)),
        "w_in": nrm(ks[3], (DEPTH, D_MODEL, IN_COLS), D_MODEL ** -0.5),
        "da_lambda_q1": nrm(ks[4], (DEPTH, DA_HEAD_DIM), DA_LAMBDA_STD),
        "da_lambda_k1": nrm(ks[5], (DEPTH, DA_HEAD_DIM), DA_LAMBDA_STD),
        "da_lambda_q2": nrm(ks[6], (DEPTH, DA_HEAD_DIM), DA_LAMBDA_STD),
        "da_lambda_k2": nrm(ks[7], (DEPTH, DA_HEAD_DIM), DA_LAMBDA_STD),
        "da_subln_g": gain(ks[8], (DEPTH, DA_V_DIM)),
        "ssm_conv_w": nrm(ks[9], (DEPTH, SSM_CONV, SSM_CONV_CH), SSM_CONV ** -0.5),
        "ssm_conv_b": nrm(ks[12], (DEPTH, SSM_CONV_CH), 0.02),
        "ssm_dt_bias": dt_bias,
        "ssm_A_log": A_log,
        "ssm_D": gain(ks[13], (DEPTH, SSM_HEADS)),
        "ssm_norm_g": gain(ks[14], (DEPTH, SSM_INNER)),
        "mem_norm_g": gain(ks[15], (DEPTH, D_MODEL)),
        "w_mem_kv": nrm(ks[16], (DEPTH, D_MODEL, 2 * XA_WIDTH), D_MODEL ** -0.5),
        "w_branch": nrm(ks[17], (DEPTH, BRANCH_IN, D_MODEL), D_MODEL ** -0.5),
        "w_out": nrm(ks[18], (DEPTH, D_MODEL, D_MODEL), D_MODEL ** -0.5 * res_scale),
        "ffn_norm_g": gain(ks[19], (DEPTH, D_MODEL)),
        "w_ffn_in": nrm(ks[20], (DEPTH, D_MODEL, 2 * FFN_HIDDEN), D_MODEL ** -0.5),
        "w_ffn_out": nrm(ks[21], (DEPTH, FFN_HIDDEN, D_MODEL), FFN_HIDDEN ** -0.5 * res_scale),
        "final_norm_g": gain(ks[22], (D_MODEL,)),
    }


def reference(x, mem, mix_norm_g, w_in, da_lambda_q1, da_lambda_k1, da_lambda_q2, da_lambda_k2,
              da_subln_g, ssm_conv_w, ssm_conv_b, ssm_dt_bias, ssm_A_log, ssm_D, ssm_norm_g,
              mem_norm_g, w_mem_kv, w_branch, w_out, ffn_norm_g, w_ffn_in, w_ffn_out, final_norm_g):
    b, s_len, d = x.shape
    for i in range(DEPTH):
        h = rms_norm(x, mix_norm_g[i])
        proj = h @ w_in[i]
        da_q, da_k, da_v, s_z, s_xbc, s_dt, xa_q, gates = split_cols(proj, IN_SIZES)

        lambda_init = 0.8 - 0.6 * math.exp(-0.3 * i)
        lam = (jnp.exp(jnp.sum(da_lambda_q1[i] * da_lambda_k1[i]).astype(jnp.float32))
               - jnp.exp(jnp.sum(da_lambda_q2[i] * da_lambda_k2[i]).astype(jnp.float32))
               + lambda_init)
        y_da = diff_attention(da_q.reshape(b, s_len, DA_HEADS, 2, DA_HEAD_DIM),
                              da_k.reshape(b, s_len, DA_HEADS, 2, DA_HEAD_DIM),
                              da_v.reshape(b, s_len, DA_HEADS, DA_V_DIM),
                              lam, lambda_init, da_subln_g[i])

        y_ssm = mamba2_bidir(s_z, s_xbc, s_dt, ssm_conv_w[i], ssm_conv_b[i], ssm_dt_bias[i],
                             ssm_A_log[i], ssm_D[i], ssm_norm_g[i])

        y_xa = mem_cross_attention(xa_q.reshape(b, s_len, XA_HEADS, XA_HEAD_DIM),
                                   rms_norm(mem, mem_norm_g[i]), w_mem_kv[i])

        g = jax.nn.sigmoid(gates.astype(jnp.float32)).astype(x.dtype).reshape(b, s_len, N_BRANCH, d)
        wb_da, wb_ssm, wb_xa = split_cols(w_branch[i].T, (DA_WIDTH, SSM_INNER, XA_WIDTH))
        merged = (g[:, :, 0] * (y_da @ wb_da.T)
                  + g[:, :, 1] * (y_ssm @ wb_ssm.T)
                  + g[:, :, 2] * (y_xa @ wb_xa.T))
        x = x + merged @ w_out[i]

        h = rms_norm(x, ffn_norm_g[i])
        gate, up = split_cols(h @ w_ffn_in[i], (FFN_HIDDEN, FFN_HIDDEN))
        x = x + (jax.nn.silu(gate) * up) @ w_ffn_out[i]
    return rms_norm(x, final_norm_g)
```

```python
import functools
import math

import numpy as np
import jax
import jax.numpy as jnp
from jax import lax
from jax.experimental import pallas as pl
from jax.experimental.pallas import tpu as pltpu

F32 = jnp.float32
BF16 = jnp.bfloat16

EPS = 1e-6
LANES = 128
DA_HEADS = 8
DA_HEAD_DIM = 64
SSM_HEADS = 32
SSM_HEAD_DIM = 64
SSM_GROUPS = 4
SSM_STATE = 128
SSM_CHUNK = 128
SSM_CONV = 5
XA_HEADS = 4
VMEM_LIMIT = 56 * 1024 * 1024


def _cparams(*sem):
    return pltpu.CompilerParams(dimension_semantics=sem, vmem_limit_bytes=VMEM_LIMIT)


def _rms(x, g):
    ms = jnp.mean(x * x, axis=-1, keepdims=True)
    return x * lax.rsqrt(ms + EPS) * g


def _nt_dot(a, b):
    return lax.dot_general(a, b, (((1,), (1,)), ((), ())), preferred_element_type=F32)


def _dot(a, b):
    return jnp.dot(a, b, preferred_element_type=F32)


def _tile(n, want):
    t = min(n, want)
    assert n % t == 0, (n, t)
    return t


def _in_proj_kernel(x_ref, g_ref, w_ref, wdt_ref, o_ref, dt_ref, h_ref):
    @pl.when(pl.program_id(1) == 0)
    def _():
        h = _rms(x_ref[...], g_ref[...]).astype(BF16)
        h_ref[...] = h
        dt_ref[...] = _dot(h, wdt_ref[...])

    o_ref[...] = _dot(h_ref[...], w_ref[...]).astype(o_ref.dtype)


def _in_proj(x2, g, w_main, w_dt):
    t, d = x2.shape
    n = w_main.shape[1]
    tm, tn = _tile(t, 1024), _tile(n, 1024)
    return pl.pallas_call(
        _in_proj_kernel,
        out_shape=(jax.ShapeDtypeStruct((t, n), BF16), jax.ShapeDtypeStruct((t, LANES), F32)),
        grid=(t // tm, n // tn),
        in_specs=[
            pl.BlockSpec((tm, d), lambda i, j: (i, 0)),
            pl.BlockSpec((1, d), lambda i, j: (0, 0)),
            pl.BlockSpec((d, tn), lambda i, j: (0, j)),
            pl.BlockSpec((d, LANES), lambda i, j: (0, 0)),
        ],
        out_specs=(
            pl.BlockSpec((tm, tn), lambda i, j: (i, j)),
            pl.BlockSpec((tm, LANES), lambda i, j: (i, 0)),
        ),
        scratch_shapes=[pltpu.VMEM((tm, d), BF16)],
        compiler_params=_cparams("parallel", "arbitrary"),
        name="in_proj",
    )(x2, g, w_main, w_dt)


def _conv_kernel(x_ref, w_ref, b_ref, o_ref):
    x = x_ref[0].astype(F32)
    s_len = x.shape[0]
    row = lax.broadcasted_iota(jnp.int32, x.shape, 0)
    pad = SSM_CONV // 2
    acc = x * w_ref[pad:pad + 1, :] + b_ref[...]
    for k in range(SSM_CONV):
        off = k - pad
        if off == 0:
            continue
        shifted = pltpu.roll(x, (-off) % s_len, axis=0)
        valid = (row >= -off) if off < 0 else (row < s_len - off)
        acc = acc + jnp.where(valid, shifted, 0.0) * w_ref[k:k + 1, :]
    o_ref[0] = (acc * jax.nn.sigmoid(acc)).astype(o_ref.dtype)


def _conv(proj3, conv_w, conv_b, col0, width):
    b, s, _ = proj3.shape
    tc = 512
    assert col0 % tc == 0 and width % tc == 0
    return pl.pallas_call(
        _conv_kernel,
        out_shape=jax.ShapeDtypeStruct((b, s, width), BF16),
        grid=(b, width // tc),
        in_specs=[
            pl.BlockSpec((1, s, tc), lambda i, j: (i, 0, col0 // tc + j)),
            pl.BlockSpec((SSM_CONV, tc), lambda i, j: (0, j)),
            pl.BlockSpec((1, tc), lambda i, j: (0, j)),
        ],
        out_specs=pl.BlockSpec((1, s, tc), lambda i, j: (i, 0, j)),
        compiler_params=_cparams("parallel", "parallel"),
        name="ssm_conv",
    )(proj3, conv_w, conv_b)


def _softplus(x):
    return jnp.maximum(x, 0.0) + jnp.log1p(jnp.exp(-jnp.abs(x)))


def _ssd_kernel(*refs, backward, fuse):
    if fuse:
        xs_ref, b_ref, c_ref, dt_ref, dtb_ref, alog_ref, yb_ref, z_ref, dsk_ref, ng_ref, o_ref, st_ref = refs
    else:
        xs_ref, b_ref, c_ref, dt_ref, dtb_ref, alog_ref, o_ref, st_ref = refs
    q = SSM_CHUNK

    @pl.when(pl.program_id(1) == 0)
    def _():
        st_ref[...] = jnp.zeros_like(st_ref)

    dt = _softplus(dt_ref[0] + dtb_ref[...])
    a = dt * (-jnp.exp(alog_ref[...]))
    row = lax.broadcasted_iota(jnp.int32, (q, LANES), 0)
    cs = a
    k = 1
    while k < q:
        if backward:
            cs = cs + jnp.where(row < q - k, pltpu.roll(cs, q - k, axis=0), 0.0)
        else:
            cs = cs + jnp.where(row >= k, pltpu.roll(cs, k, axis=0), 0.0)
        k *= 2
    cs_t = cs.T
    dt_t = dt.T
    end = 0 if backward else q - 1
    w_state_t = dt_t * jnp.exp(cs_t[:, end:end + 1] - cs_t)

    li = lax.broadcasted_iota(jnp.int32, (q, q), 0)
    si = lax.broadcasted_iota(jnp.int32, (q, q), 1)
    tri = (li <= si) if backward else (li >= si)
    lo = lax.broadcasted_iota(jnp.int32, (q, LANES), 1) < SSM_HEAD_DIM
    lane0 = SSM_HEADS if backward else 0
    pairs_per_group = SSM_HEADS // SSM_GROUPS // 2
    gw = SSM_HEADS // SSM_GROUPS * SSM_HEAD_DIM

    def blockdiag(v):
        zero = jnp.zeros_like(v)
        return jnp.concatenate([jnp.where(lo, v, zero), jnp.where(lo, zero, v)], axis=0)

    for g in range(SSM_GROUPS):
        bg = b_ref[0, :, g * SSM_STATE:(g + 1) * SSM_STATE]
        cg = c_ref[0, :, g * SSM_STATE:(g + 1) * SSM_STATE]
        cb = _nt_dot(cg, bg)
        bg_t = bg.astype(F32).T
        cg_f = cg.astype(F32)
        ys = []
        for pp in range(pairs_per_group):
            p = g * pairs_per_group + pp
            sl = slice(p * LANES, (p + 1) * LANES)
            xbd = blockdiag(xs_ref[0, :, sl])
            m_l, b_l, c_l, cd = [], [], [], []
            for h in (2 * p, 2 * p + 1):
                r = lane0 + h
                col = cs[:, r:r + 1]
                decay = jnp.exp(jnp.where(tri, col - cs_t[r:r + 1, :], -jnp.inf))
                m_l.append((cb * decay * dt_t[r:r + 1, :]).astype(BF16))
                b_l.append((bg_t * w_state_t[r:r + 1, :]).astype(BF16))
                e_in = jnp.exp(col)
                c_l.append((cg_f * e_in).astype(BF16))
                cd.append(e_in[end:end + 1, :])
            prev = st_ref[:, sl]
            pbd = blockdiag(prev.astype(BF16))
            y = _dot(jnp.concatenate(m_l, axis=1), xbd) + _dot(jnp.concatenate(c_l, axis=1), pbd)
            st_new = _dot(jnp.concatenate(b_l, axis=1), xbd)
            st_ref[:, sl] = prev * jnp.where(lo, cd[0], cd[1]) + st_new
            if fuse:
                ys.append(y)
            else:
                o_ref[0, :, sl] = y
        if fuse:
            gs = slice(g * gw, (g + 1) * gw)
            yg = jnp.concatenate(ys, axis=1) + yb_ref[0, :, gs] + xs_ref[0, :, gs].astype(F32) * dsk_ref[:, gs]
            zg = z_ref[0, :, gs].astype(F32)
            yg = yg * (zg * jax.nn.sigmoid(zg))
            o_ref[0, :, gs] = _rms(yg, ng_ref[:, gs]).astype(o_ref.dtype)


def _ssd(xbc, dt_raw, dt_bias, a_log, *, backward, fused=None):
    b, s, _ = xbc.shape
    q = SSM_CHUNK
    nc = s // q
    inner = SSM_HEADS * SSM_HEAD_DIM
    bc = SSM_GROUPS * SSM_STATE
    assert inner % bc == 0
    cix = (lambda c: nc - 1 - c) if backward else (lambda c: c)
    in_specs = [
        pl.BlockSpec((1, q, inner), lambda i, c: (i, cix(c), 0)),
        pl.BlockSpec((1, q, bc), lambda i, c: (i, cix(c), inner // bc)),
        pl.BlockSpec((1, q, bc), lambda i, c: (i, cix(c), inner // bc + 1)),
        pl.BlockSpec((1, q, LANES), lambda i, c: (i, cix(c), 0)),
        pl.BlockSpec((1, LANES), lambda i, c: (0, 0)),
        pl.BlockSpec((1, LANES), lambda i, c: (0, 0)),
    ]
    args = [xbc, xbc, xbc, dt_raw, dt_bias, a_log]
    if fused is not None:
        y_bwd, proj3, d_skip, norm_g = fused
        in_specs += [
            pl.BlockSpec((1, q, inner), lambda i, c: (i, cix(c), 0)),
            pl.BlockSpec((1, q, inner), lambda i, c: (i, cix(c), 0)),
            pl.BlockSpec((1, inner), lambda i, c: (0, 0)),
            pl.BlockSpec((1, inner), lambda i, c: (0, 0)),
        ]
        args += [y_bwd, proj3, d_skip, norm_g]
    return pl.pallas_call(
        functools.partial(_ssd_kernel, backward=backward, fuse=fused is not None),
        out_shape=jax.ShapeDtypeStruct((b, s, inner), BF16 if fused is not None else F32),
        grid=(b, nc),
        in_specs=in_specs,
        out_specs=pl.BlockSpec((1, q, inner), lambda i, c: (i, cix(c), 0)),
        scratch_shapes=[pltpu.VMEM((SSM_STATE, inner), F32)],
        compiler_params=_cparams("parallel", "arbitrary"),
        name="ssd_bwd" if backward else "ssd_fwd",
    )(*args)


def _diff_attn_kernel(slopes_ref, lam_ref, q_ref, k_ref, v_ref, g_ref, o_ref, *, out_scale):
    tq = q_ref.shape[1]
    s_len = k_ref.shape[1]
    h = pl.program_id(1)
    qs = q_ref[0] * (DA_HEAD_DIM ** -0.5)
    lo = lax.broadcasted_iota(jnp.int32, qs.shape, 1) < DA_HEAD_DIM
    zero = jnp.zeros_like(qs)
    k = k_ref[0]
    qpos = pl.program_id(2) * tq + lax.broadcasted_iota(jnp.int32, (tq, s_len), 0)
    kpos = lax.broadcasted_iota(jnp.int32, (tq, s_len), 1)
    bias = slopes_ref[h] * jnp.abs(qpos - kpos).astype(F32)

    def softmax_parts(qm):
        s = _nt_dot(qm, k) - bias
        p = jnp.exp(s - jnp.max(s, axis=-1, keepdims=True))
        return p, jnp.sum(p, axis=-1, keepdims=True)

    p1, l1 = softmax_parts(jnp.where(lo, qs, zero))
    p2, l2 = softmax_parts(jnp.where(lo, zero, qs))
    attn = p1 * (1.0 / l1) - p2 * (lam_ref[0] / l2)
    o = _dot(attn.astype(BF16), v_ref[0])
    o_ref[0] = (_rms(o, g_ref[...]) * out_scale).astype(o_ref.dtype)


def _diff_attn(proj3, slopes, lam, sub_g, col_q, col_k, col_v, lambda_init):
    b, s, _ = proj3.shape
    w = 2 * DA_HEAD_DIM
    tq = _tile(s, 256)
    smem = pl.BlockSpec(memory_space=pltpu.SMEM)
    return pl.pallas_call(
        functools.partial(_diff_attn_kernel, out_scale=1.0 - lambda_init),
        out_shape=jax.ShapeDtypeStruct((b, s, DA_HEADS * w), BF16),
        grid=(b, DA_HEADS, s // tq),
        in_specs=[
            smem, smem,
            pl.BlockSpec((1, tq, w), lambda i, h, t: (i, t, col_q // w + h)),
            pl.BlockSpec((1, s, w), lambda i, h, t: (i, 0, col_k // w + h)),
            pl.BlockSpec((1, s, w), lambda i, h, t: (i, 0, col_v // w + h)),
            pl.BlockSpec((1, w), lambda i, h, t: (0, 0)),
        ],
        out_specs=pl.BlockSpec((1, tq, w), lambda i, h, t: (i, t, h)),
        compiler_params=_cparams("parallel", "parallel", "parallel"),
        name="diff_attn",
    )(slopes, lam, proj3, proj3, proj3, sub_g)


def _mem_kv_kernel(m_ref, g_ref, w_ref, o_ref):
    mn = _rms(m_ref[...], g_ref[0]).astype(BF16)
    o_ref[0] = _dot(mn, w_ref[0]).astype(o_ref.dtype)


def _mem_kv(mem2, mem_g, w_kv):
    rows, d = mem2.shape
    depth, _, n = w_kv.shape
    tn = _tile(n, 1024)
    return pl.pallas_call(
        _mem_kv_kernel,
        out_shape=jax.ShapeDtypeStruct((depth, rows, n), BF16),
        grid=(depth, n // tn),
        in_specs=[
            pl.BlockSpec((rows, d), lambda l, j: (0, 0)),
            pl.BlockSpec((1, 1, d), lambda l, j: (l, 0, 0)),
            pl.BlockSpec((1, d, tn), lambda l, j: (l, 0, j)),
        ],
        out_specs=pl.BlockSpec((1, rows, tn), lambda l, j: (l, 0, j)),
        compiler_params=_cparams("parallel", "parallel"),
        name="mem_kv",
    )(mem2, mem_g, w_kv)


def _xattn_kernel(q_ref, kv_ref, o_ref):
    xw = q_ref.shape[2]
    hd = xw // XA_HEADS
    for h in range(XA_HEADS):
        sl = slice(h * hd, (h + 1) * hd)
        qh = q_ref[0, :, sl] * (hd ** -0.5)
        s = _nt_dot(qh, kv_ref[0, :, sl])
        p = jnp.exp(s - jnp.max(s, axis=-1, keepdims=True))
        p = p * (1.0 / jnp.sum(p, axis=-1, keepdims=True))
        o_ref[0, :, sl] = _dot(p.astype(BF16), kv_ref[0, :, xw + h * hd:xw + (h + 1) * hd]).astype(o_ref.dtype)


def _xattn(proj3, kv3, col_q, xw):
    b, s, _ = proj3.shape
    m = kv3.shape[1]
    tq = _tile(s, 512)
    return pl.pallas_call(
        _xattn_kernel,
        out_shape=jax.ShapeDtypeStruct((b, s, xw), BF16),
        grid=(b, s // tq),
        in_specs=[
            pl.BlockSpec((1, tq, xw), lambda i, t: (i, t, col_q // xw)),
            pl.BlockSpec((1, m, 2 * xw), lambda i, t: (i, 0, 0)),
        ],
        out_specs=pl.BlockSpec((1, tq, xw), lambda i, t: (i, t, 0)),
        compiler_params=_cparams("parallel", "parallel"),
        name="mem_xattn",
    )(proj3, kv3)


def _merge_kernel(yda_ref, yssm_ref, yxa_ref, gt_ref, x_ref, wb_ref, wo_ref, o_ref):
    d = x_ref.shape[1]
    n_da, n_ssm = yda_ref.shape[1], yssm_ref.shape[1]
    gate = jax.nn.sigmoid(gt_ref[...].astype(F32))
    merged = (gate[:, :d] * _dot(yda_ref[...], wb_ref[:n_da, :])
              + gate[:, d:2 * d] * _dot(yssm_ref[...], wb_ref[n_da:n_da + n_ssm, :])
              + gate[:, 2 * d:] * _dot(yxa_ref[...], wb_ref[n_da + n_ssm:, :]))
    o_ref[...] = x_ref[...] + _dot(merged.astype(BF16), wo_ref[...])


def _merge(y_da, y_ssm, y_xa, proj2, col_gates, x2, w_branch, w_out):
    t, d = x2.shape
    tm = _tile(t, 512)
    assert col_gates % (3 * d) == 0
    row = lambda i: (i, 0)
    whole = lambda i: (0, 0)
    return pl.pallas_call(
        _merge_kernel,
        out_shape=jax.ShapeDtypeStruct((t, d), F32),
        grid=(t // tm,),
        in_specs=[
            pl.BlockSpec((tm, y_da.shape[1]), row),
            pl.BlockSpec((tm, y_ssm.shape[1]), row),
            pl.BlockSpec((tm, y_xa.shape[1]), row),
            pl.BlockSpec((tm, 3 * d), lambda i: (i, col_gates // (3 * d))),
            pl.BlockSpec((tm, d), row),
            pl.BlockSpec(w_branch.shape, whole),
            pl.BlockSpec(w_out.shape, whole),
        ],
        out_specs=pl.BlockSpec((tm, d), row),
        compiler_params=_cparams("parallel"),
        name="merge_out",
    )(y_da, y_ssm, y_xa, proj2, x2, w_branch, w_out)


def _ffn_kernel(x_ref, g_ref, wg_ref, wu_ref, wo_ref, fg_ref, o_ref, h_ref, acc_ref, *, final_norm):
    j = pl.program_id(1)

    @pl.when(j == 0)
    def _():
        h_ref[...] = _rms(x_ref[...], g_ref[...]).astype(BF16)
        acc_ref[...] = jnp.zeros_like(acc_ref)

    h = h_ref[...]
    gate = _dot(h, wg_ref[...])
    up = _dot(h, wu_ref[...])
    act = (gate * jax.nn.sigmoid(gate) * up).astype(BF16)
    acc_ref[...] += _dot(act, wo_ref[...])

    @pl.when(j == pl.num_programs(1) - 1)
    def _():
        y = x_ref[...] + acc_ref[...]
        if final_norm:
            y = _rms(y, fg_ref[...])
        o_ref[...] = y


def _ffn(x2, g, w_in, w_out, final_g, final_norm):
    t, d = x2.shape
    hid = w_out.shape[0]
    tm, th = _tile(t, 1024), 256
    nh = hid // th
    assert hid % th == 0
    return pl.pallas_call(
        functools.partial(_ffn_kernel, final_norm=final_norm),
        out_shape=jax.ShapeDtypeStruct((t, d), F32),
        grid=(t // tm, nh),
        in_specs=[
            pl.BlockSpec((tm, d), lambda i, j: (i, 0)),
            pl.BlockSpec((1, d), lambda i, j: (0, 0)),
            pl.BlockSpec((d, th), lambda i, j: (0, j)),
            pl.BlockSpec((d, th), lambda i, j: (0, nh + j)),
            pl.BlockSpec((th, d), lambda i, j: (j, 0)),
            pl.BlockSpec((1, d), lambda i, j: (0, 0)),
        ],
        out_specs=pl.BlockSpec((tm, d), lambda i, j: (i, 0)),
        scratch_shapes=[pltpu.VMEM((tm, d), BF16), pltpu.VMEM((tm, d), F32)],
        compiler_params=_cparams("parallel", "arbitrary"),
        name="ffn",
    )(x2, g, w_in, w_in, w_out, final_g)


def _alibi_slopes(n_heads):
    start = 2.0 ** (-8.0 / n_heads)
    return np.array([start ** (i + 1) for i in range(n_heads)], dtype=np.float32)


def kernel(x, mem, mix_norm_g, w_in, da_lambda_q1, da_lambda_k1, da_lambda_q2, da_lambda_k2, da_subln_g, ssm_conv_w, ssm_conv_b, ssm_dt_bias, ssm_A_log, ssm_D, ssm_norm_g, mem_norm_g, w_mem_kv, w_branch, w_out, ffn_norm_g, w_ffn_in, w_ffn_out, final_norm_g):
    b, s, d = x.shape
    depth = w_in.shape[0]
    t = b * s
    m = mem.shape[1]
    da_cols = DA_HEADS * 2 * DA_HEAD_DIM
    inner = SSM_HEADS * SSM_HEAD_DIM
    conv_ch = inner + 2 * SSM_GROUPS * SSM_STATE
    n_dt = 2 * SSM_HEADS
    xw = d

    o_q, o_k, o_v = 0, da_cols, 2 * da_cols
    o_z = 3 * da_cols
    o_xbc = o_z + inner
    o_dt = o_xbc + conv_ch
    o_xq = o_dt + n_dt
    o_gt = o_xq + xw
    w_main = jnp.concatenate(
        [w_in[:, :, o_z:o_dt], w_in[:, :, o_q:o_z], w_in[:, :, o_xq:]], axis=2).astype(BF16)
    w_dt = jnp.pad(w_in[:, :, o_dt:o_xq], ((0, 0), (0, 0), (0, LANES - n_dt))).astype(BF16)
    c_z, c_xbc = 0, inner
    c_q = inner + conv_ch
    c_k, c_v = c_q + da_cols, c_q + 2 * da_cols
    c_xq = c_q + 3 * da_cols
    c_gt = c_xq + xw

    w_kv_b = w_mem_kv.astype(BF16)
    w_branch_b = w_branch.astype(BF16)
    w_out_b = w_out.astype(BF16)
    w_ffn_in_b = w_ffn_in.astype(BF16)
    w_ffn_out_b = w_ffn_out.astype(BF16)

    pad_lanes = lambda v: jnp.pad(v.reshape(depth, 1, n_dt), ((0, 0), (0, 0), (0, LANES - n_dt)))
    dt_bias_p = pad_lanes(ssm_dt_bias.astype(F32))
    a_log_p = pad_lanes(ssm_A_log.astype(F32))
    d_skip = jnp.repeat(ssm_D.astype(F32), SSM_HEAD_DIM, axis=1).reshape(depth, 1, inner)
    slopes = jnp.asarray(_alibi_slopes(DA_HEADS))

    kv_all = _mem_kv(mem.reshape(b * m, d), mem_norm_g.reshape(depth, 1, d), w_kv_b)

    x2 = x.reshape(t, d)
    for i in range(depth):
        lambda_init = 0.8 - 0.6 * math.exp(-0.3 * i)
        lam = (jnp.exp(jnp.sum(da_lambda_q1[i] * da_lambda_k1[i]).astype(F32))
               - jnp.exp(jnp.sum(da_lambda_q2[i] * da_lambda_k2[i]).astype(F32))
               + lambda_init).reshape(1)

        proj2, dt_raw = _in_proj(x2, mix_norm_g[i].reshape(1, d), w_main[i], w_dt[i])
        proj3 = proj2.reshape(b, s, -1)
        dt3 = dt_raw.reshape(b, s, LANES)

        xbc = _conv(proj3, ssm_conv_w[i], ssm_conv_b[i].reshape(1, conv_ch), c_xbc, conv_ch)
        y_bwd = _ssd(xbc, dt3, dt_bias_p[i], a_log_p[i], backward=True)
        y_ssm = _ssd(xbc, dt3, dt_bias_p[i], a_log_p[i], backward=False,
                     fused=(y_bwd, proj3, d_skip[i], ssm_norm_g[i].reshape(1, inner)))

        y_da = _diff_attn(proj3, slopes, lam, da_subln_g[i].reshape(1, 2 * DA_HEAD_DIM),
                          c_q, c_k, c_v, lambda_init)
        y_xa = _xattn(proj3, kv_all[i].reshape(b, m, 2 * xw), c_xq, xw)

        x2 = _merge(y_da.reshape(t, -1), y_ssm.reshape(t, inner), y_xa.reshape(t, xw),
                    proj2, c_gt, x2, w_branch_b[i], w_out_b[i])
        x2 = _ffn(x2, ffn_norm_g[i].reshape(1, d), w_ffn_in_b[i], w_ffn_out_b[i],
                  final_norm_g.reshape(1, d), final_norm=(i == depth - 1))
    return x2.reshape(b, s, d)
```

```python
import functools
import math

import numpy as np
import jax
import jax.numpy as jnp
from jax import lax
from jax.experimental import pallas as pl
from jax.experimental.pallas import tpu as pltpu

F32 = jnp.float32
BF16 = jnp.bfloat16

EPS = 1e-6
LOG2E = math.log2(math.e)
LANES = 128
BF16_SUBLANES = 16
DA_HEADS = 8
DA_HEAD_DIM = 64
DA_KEY_CHUNK = 256
SSM_HEADS = 32
SSM_HEAD_DIM = 64
SSM_GROUPS = 4
SSM_STATE = 128
SSM_CHUNK = 128
SSM_CONV = 5
XA_HEADS = 4
VMEM_LIMIT = 56 * 1024 * 1024


def _cparams(*sem):
    return pltpu.CompilerParams(dimension_semantics=sem, vmem_limit_bytes=VMEM_LIMIT)


def _rms(x, g):
    ms = jnp.mean(x * x, axis=-1, keepdims=True)
    return x * lax.rsqrt(ms + EPS) * g


def _nt_dot(a, b):
    return lax.dot_general(a, b, (((1,), (1,)), ((), ())), preferred_element_type=F32)


def _dot(a, b):
    return jnp.dot(a, b, preferred_element_type=F32)


def _tile(n, want):
    t = min(n, want)
    assert n % t == 0, (n, t)
    return t


def _in_proj_kernel(x_ref, g_ref, w_ref, wdt_ref, o_ref, dt_ref, h_ref):
    @pl.when(pl.program_id(1) == 0)
    def _():
        h = _rms(x_ref[...], g_ref[...]).astype(BF16)
        h_ref[...] = h
        dt_ref[...] = _dot(h, wdt_ref[...])

    o_ref[...] = _dot(h_ref[...], w_ref[...]).astype(o_ref.dtype)


def _in_proj(x2, g, w_main, w_dt):
    t, d = x2.shape
    n = w_main.shape[1]
    tm, tn = _tile(t, 1024), _tile(n, 1024)
    return pl.pallas_call(
        _in_proj_kernel,
        out_shape=(jax.ShapeDtypeStruct((t, n), BF16), jax.ShapeDtypeStruct((t, LANES), F32)),
        grid=(t // tm, n // tn),
        in_specs=[
            pl.BlockSpec((tm, d), lambda i, j: (i, 0)),
            pl.BlockSpec((1, d), lambda i, j: (0, 0)),
            pl.BlockSpec((d, tn), lambda i, j: (0, j)),
            pl.BlockSpec((d, LANES), lambda i, j: (0, 0)),
        ],
        out_specs=(
            pl.BlockSpec((tm, tn), lambda i, j: (i, j)),
            pl.BlockSpec((tm, LANES), lambda i, j: (i, 0)),
        ),
        scratch_shapes=[pltpu.VMEM((tm, d), BF16)],
        compiler_params=_cparams("parallel", "arbitrary"),
        name="in_proj",
    )(x2, g, w_main, w_dt)


def _conv_kernel(x_ref, w_ref, b_ref, o_ref):
    x = x_ref[0].astype(F32)
    s_len = x.shape[0]
    row = lax.broadcasted_iota(jnp.int32, x.shape, 0)
    pad = SSM_CONV // 2
    acc = x * w_ref[pad:pad + 1, :] + b_ref[...]
    for k in range(SSM_CONV):
        off = k - pad
        if off == 0:
            continue
        shifted = pltpu.roll(x, (-off) % s_len, axis=0)
        valid = (row >= -off) if off < 0 else (row < s_len - off)
        acc = acc + jnp.where(valid, shifted, 0.0) * w_ref[k:k + 1, :]
    o_ref[0] = (acc * jax.nn.sigmoid(acc)).astype(o_ref.dtype)


def _conv(proj3, conv_w, conv_b, col0, width):
    b, s, _ = proj3.shape
    tc = 512
    assert col0 % tc == 0 and width % tc == 0
    return pl.pallas_call(
        _conv_kernel,
        out_shape=jax.ShapeDtypeStruct((b, s, width), BF16),
        grid=(b, width // tc),
        in_specs=[
            pl.BlockSpec((1, s, tc), lambda i, j: (i, 0, col0 // tc + j)),
            pl.BlockSpec((SSM_CONV, tc), lambda i, j: (0, j)),
            pl.BlockSpec((1, tc), lambda i, j: (0, j)),
        ],
        out_specs=pl.BlockSpec((1, s, tc), lambda i, j: (i, 0, j)),
        compiler_params=_cparams("parallel", "parallel"),
        name="ssm_conv",
    )(proj3, conv_w, conv_b)


def _softplus(x):
    return jnp.maximum(x, 0.0) + jnp.log1p(jnp.exp(-jnp.abs(x)))


def _ssd_kernel(*refs, backward, fuse):
    if fuse:
        xs_ref, b_ref, c_ref, dt_ref, dtb_ref, alog_ref, yb_ref, z_ref, dsk_ref, ng_ref, o_ref, st_ref = refs
    else:
        xs_ref, b_ref, c_ref, dt_ref, dtb_ref, alog_ref, o_ref, st_ref = refs
    q = SSM_CHUNK

    @pl.when(pl.program_id(1) == 0)
    def _():
        st_ref[...] = jnp.zeros_like(st_ref)

    dt = _softplus(dt_ref[0] + dtb_ref[...])
    a = dt * (-jnp.exp(alog_ref[...]))
    row = lax.broadcasted_iota(jnp.int32, (q, LANES), 0)
    cs = a
    k = 1
    while k < q:
        if backward:
            cs = cs + jnp.where(row < q - k, pltpu.roll(cs, q - k, axis=0), 0.0)
        else:
            cs = cs + jnp.where(row >= k, pltpu.roll(cs, k, axis=0), 0.0)
        k *= 2
    cs_t = cs.T
    dt_t = dt.T
    end = 0 if backward else q - 1
    w_state_t = dt_t * jnp.exp(cs_t[:, end:end + 1] - cs_t)

    li = lax.broadcasted_iota(jnp.int32, (q, q), 0)
    si = lax.broadcasted_iota(jnp.int32, (q, q), 1)
    tri = (li <= si) if backward else (li >= si)
    lo = lax.broadcasted_iota(jnp.int32, (q, LANES), 1) < SSM_HEAD_DIM
    lane0 = SSM_HEADS if backward else 0
    pairs_per_group = SSM_HEADS // SSM_GROUPS // 2
    gw = SSM_HEADS // SSM_GROUPS * SSM_HEAD_DIM

    def blockdiag(v):
        zero = jnp.zeros_like(v)
        return jnp.concatenate([jnp.where(lo, v, zero), jnp.where(lo, zero, v)], axis=0)

    for g in range(SSM_GROUPS):
        bg = b_ref[0, :, g * SSM_STATE:(g + 1) * SSM_STATE]
        cg = c_ref[0, :, g * SSM_STATE:(g + 1) * SSM_STATE]
        cb = _nt_dot(cg, bg)
        bg_t = bg.astype(F32).T
        cg_f = cg.astype(F32)
        ys = []
        for pp in range(pairs_per_group):
            p = g * pairs_per_group + pp
            sl = slice(p * LANES, (p + 1) * LANES)
            xbd = blockdiag(xs_ref[0, :, sl])
            m_l, b_l, c_l, cd = [], [], [], []
            for h in (2 * p, 2 * p + 1):
                r = lane0 + h
                col = cs[:, r:r + 1]
                decay = jnp.exp(jnp.where(tri, col - cs_t[r:r + 1, :], -jnp.inf))
                m_l.append((cb * decay * dt_t[r:r + 1, :]).astype(BF16))
                b_l.append((bg_t * w_state_t[r:r + 1, :]).astype(BF16))
                e_in = jnp.exp(col)
                c_l.append((cg_f * e_in).astype(BF16))
                cd.append(e_in[end:end + 1, :])
            prev = st_ref[:, sl]
            pbd = blockdiag(prev.astype(BF16))
            y = _dot(jnp.concatenate(m_l, axis=1), xbd) + _dot(jnp.concatenate(c_l, axis=1), pbd)
            st_new = _dot(jnp.concatenate(b_l, axis=1), xbd)
            st_ref[:, sl] = prev * jnp.where(lo, cd[0], cd[1]) + st_new
            if fuse:
                ys.append(y)
            else:
                o_ref[0, :, sl] = y
        if fuse:
            gs = slice(g * gw, (g + 1) * gw)
            yg = jnp.concatenate(ys, axis=1) + yb_ref[0, :, gs] + xs_ref[0, :, gs].astype(F32) * dsk_ref[:, gs]
            zg = z_ref[0, :, gs].astype(F32)
            yg = yg * (zg * jax.nn.sigmoid(zg))
            o_ref[0, :, gs] = _rms(yg, ng_ref[:, gs]).astype(o_ref.dtype)


def _ssd(xbc, dt_raw, dt_bias, a_log, *, backward, fused=None):
    b, s, _ = xbc.shape
    q = SSM_CHUNK
    nc = s // q
    inner = SSM_HEADS * SSM_HEAD_DIM
    bc = SSM_GROUPS * SSM_STATE
    assert inner % bc == 0
    cix = (lambda c: nc - 1 - c) if backward else (lambda c: c)
    in_specs = [
        pl.BlockSpec((1, q, inner), lambda i, c: (i, cix(c), 0)),
        pl.BlockSpec((1, q, bc), lambda i, c: (i, cix(c), inner // bc)),
        pl.BlockSpec((1, q, bc), lambda i, c: (i, cix(c), inner // bc + 1)),
        pl.BlockSpec((1, q, LANES), lambda i, c: (i, cix(c), 0)),
        pl.BlockSpec((1, LANES), lambda i, c: (0, 0)),
        pl.BlockSpec((1, LANES), lambda i, c: (0, 0)),
    ]
    args = [xbc, xbc, xbc, dt_raw, dt_bias, a_log]
    if fused is not None:
        y_bwd, proj3, d_skip, norm_g = fused
        in_specs += [
            pl.BlockSpec((1, q, inner), lambda i, c: (i, cix(c), 0)),
            pl.BlockSpec((1, q, inner), lambda i, c: (i, cix(c), 0)),
            pl.BlockSpec((1, inner), lambda i, c: (0, 0)),
            pl.BlockSpec((1, inner), lambda i, c: (0, 0)),
        ]
        args += [y_bwd, proj3, d_skip, norm_g]
    return pl.pallas_call(
        functools.partial(_ssd_kernel, backward=backward, fuse=fused is not None),
        out_shape=jax.ShapeDtypeStruct((b, s, inner), BF16 if fused is not None else F32),
        grid=(b, nc),
        in_specs=in_specs,
        out_specs=pl.BlockSpec((1, q, inner), lambda i, c: (i, cix(c), 0)),
        scratch_shapes=[pltpu.VMEM((SSM_STATE, inner), F32)],
        compiler_params=_cparams("parallel", "arbitrary"),
        name="ssd_bwd" if backward else "ssd_fwd",
    )(*args)


def _diff_attn_kernel(slopes_ref, lam_ref, q_ref, k_ref, v_ref, g_ref, o_ref, bias_ref, vext_ref, s_ref, m_ref, a_ref,
                      *, out_scale):
    tq = q_ref.shape[1] // 2
    s_len = k_ref.shape[1]
    w = v_ref.shape[2]
    npb = s_len // (2 * tq)
    h, g = pl.program_id(0), pl.program_id(1)
    n_pairs = pl.num_programs(1) - 1
    ga = jnp.minimum(g, n_pairs - 1)
    ba, pa = ga // npb, ga % npb
    bb = jnp.maximum(g - 1, 0) // npb

    @pl.when(g == 0)
    def _():
        s_ref[...] = jnp.zeros_like(s_ref)
        m_ref[...] = jnp.zeros_like(m_ref)
        a_ref[...] = jnp.ones_like(a_ref)

    @pl.when(jnp.logical_and(ba == 0, g < n_pairs))
    def _():
        kpos = lax.broadcasted_iota(jnp.int32, (s_len, tq), 0)
        for half in range(2):
            qpos = (2 * pa + half) * tq + lax.broadcasted_iota(jnp.int32, (s_len, tq), 1)
            bias_ref[2 * pa + half] = (slopes_ref[h] * LOG2E) * jnp.abs(qpos - kpos).astype(F32)

    @pl.when(jnp.logical_and(pa == 0, g < n_pairs))
    def _():
        vext_ref[ba % 2, :w, :] = v_ref[0].astype(F32).T.astype(BF16)
        vext_ref[ba % 2, w:, :] = jnp.ones((vext_ref.shape[1] - w, s_len), BF16)

    def raw_scores(half):
        q = q_ref[0, half * tq:(half + 1) * tq, :]
        lo = lax.broadcasted_iota(jnp.int32, q.shape, 1) < DA_HEAD_DIM
        zero = jnp.zeros_like(q)
        return _nt_dot(k_ref[0], jnp.concatenate([jnp.where(lo, q, zero), jnp.where(lo, zero, q)], axis=0))

    def biased(raw, half):
        bias = bias_ref[2 * pa + half]
        s = jnp.concatenate([raw[:, :tq] - bias, raw[:, tq:] - bias], axis=1)
        return s, jnp.max(s, axis=0, keepdims=True)

    def epilogue(a):
        a1, a2 = a[:, :tq], a[:, tq:]
        o_t = a1[:w] * (1.0 / a1[w:w + 1]) - a2[:w] * (lam_ref[0] / a2[w:w + 1])
        return (_rms(o_t.T, g_ref[...]) * out_scale).astype(o_ref.dtype)

    raw0 = raw_scores(0)
    p_prev = jnp.exp2(s_ref[...] - m_ref[...]).astype(BF16)
    o_ref[0, :tq, :] = epilogue(a_ref[...])
    a_prev = _dot(vext_ref[bb % 2], p_prev)
    s0, m0 = biased(raw0, 0)
    raw1 = raw_scores(1)
    p0 = jnp.exp2(s0 - m0).astype(BF16)
    o_ref[0, tq:, :] = epilogue(a_prev)
    a_ref[...] = _dot(vext_ref[ba % 2], p0)
    s1, m1 = biased(raw1, 1)
    s_ref[...] = s1
    m_ref[...] = m1


def _diff_attn(proj3, slopes, lam, sub_g, col_q, col_k, col_v, lambda_init):
    b, s, _ = proj3.shape
    w = 2 * DA_HEAD_DIM
    tq = _tile(s, 512) // 2
    npb = s // (2 * tq)
    n_pairs = b * npb
    smem = pl.BlockSpec(memory_space=pltpu.SMEM)

    def pair_in(g):
        ga = jnp.minimum(g, n_pairs - 1)
        return ga // npb, ga % npb

    def pair_out(g):
        gb = jnp.maximum(g - 1, 0)
        return gb // npb, gb % npb

    return pl.pallas_call(
        functools.partial(_diff_attn_kernel, out_scale=1.0 - lambda_init),
        out_shape=jax.ShapeDtypeStruct((b, s, DA_HEADS * w), BF16),
        grid=(DA_HEADS, n_pairs + 1),
        in_specs=[
            smem, smem,
            pl.BlockSpec((1, 2 * tq, w), lambda h, g: (*pair_in(g), col_q // w + h)),
            pl.BlockSpec((1, s, w), lambda h, g: (pair_in(g)[0], 0, col_k // w + h)),
            pl.BlockSpec((1, s, w), lambda h, g: (pair_in(g)[0], 0, col_v // w + h)),
            pl.BlockSpec((1, w), lambda h, g: (0, 0)),
        ],
        out_specs=pl.BlockSpec((1, 2 * tq, w), lambda h, g: (*pair_out(g), h)),
        scratch_shapes=[
            pltpu.VMEM((s // tq, s, tq), F32),
            pltpu.VMEM((2, w + BF16_SUBLANES, s), BF16),
            pltpu.VMEM((s, 2 * tq), F32),
            pltpu.VMEM((1, 2 * tq), F32),
            pltpu.VMEM((w + BF16_SUBLANES, 2 * tq), F32),
        ],
        compiler_params=_cparams("parallel", "arbitrary"),
        name="diff_attn",
    )(slopes, lam, proj3, proj3, proj3, sub_g)


def _mem_kv_kernel(m_ref, g_ref, w_ref, o_ref):
    mn = _rms(m_ref[...], g_ref[0]).astype(BF16)
    o_ref[0] = _dot(mn, w_ref[0]).astype(o_ref.dtype)


def _mem_kv(mem2, mem_g, w_kv):
    rows, d = mem2.shape
    depth, _, n = w_kv.shape
    tn = _tile(n, 1024)
    return pl.pallas_call(
        _mem_kv_kernel,
        out_shape=jax.ShapeDtypeStruct((depth, rows, n), BF16),
        grid=(depth, n // tn),
        in_specs=[
            pl.BlockSpec((rows, d), lambda l, j: (0, 0)),
            pl.BlockSpec((1, 1, d), lambda l, j: (l, 0, 0)),
            pl.BlockSpec((1, d, tn), lambda l, j: (l, 0, j)),
        ],
        out_specs=pl.BlockSpec((1, rows, tn), lambda l, j: (l, 0, j)),
        compiler_params=_cparams("parallel", "parallel"),
        name="mem_kv",
    )(mem2, mem_g, w_kv)


def _xattn_kernel(q_ref, kv_ref, o_ref):
    xw = q_ref.shape[2]
    hd = xw // XA_HEADS
    for h in range(XA_HEADS):
        sl = slice(h * hd, (h + 1) * hd)
        qh = q_ref[0, :, sl] * (hd ** -0.5)
        s = _nt_dot(qh, kv_ref[0, :, sl])
        p = jnp.exp(s - jnp.max(s, axis=-1, keepdims=True))
        p = p * (1.0 / jnp.sum(p, axis=-1, keepdims=True))
        o_ref[0, :, sl] = _dot(p.astype(BF16), kv_ref[0, :, xw + h * hd:xw + (h + 1) * hd]).astype(o_ref.dtype)


def _xattn(proj3, kv3, col_q, xw):
    b, s, _ = proj3.shape
    m = kv3.shape[1]
    tq = _tile(s, 512)
    return pl.pallas_call(
        _xattn_kernel,
        out_shape=jax.ShapeDtypeStruct((b, s, xw), BF16),
        grid=(b, s // tq),
        in_specs=[
            pl.BlockSpec((1, tq, xw), lambda i, t: (i, t, col_q // xw)),
            pl.BlockSpec((1, m, 2 * xw), lambda i, t: (i, 0, 0)),
        ],
        out_specs=pl.BlockSpec((1, tq, xw), lambda i, t: (i, t, 0)),
        compiler_params=_cparams("parallel", "parallel"),
        name="mem_xattn",
    )(proj3, kv3)


def _merge_kernel(yda_ref, yssm_ref, yxa_ref, gt_ref, x_ref, wb_ref, wo_ref, o_ref):
    d = x_ref.shape[1]
    n_da, n_ssm = yda_ref.shape[1], yssm_ref.shape[1]
    gate = jax.nn.sigmoid(gt_ref[...].astype(F32))
    merged = (gate[:, :d] * _dot(yda_ref[...], wb_ref[:n_da, :])
              + gate[:, d:2 * d] * _dot(yssm_ref[...], wb_ref[n_da:n_da + n_ssm, :])
              + gate[:, 2 * d:] * _dot(yxa_ref[...], wb_ref[n_da + n_ssm:, :]))
    o_ref[...] = x_ref[...] + _dot(merged.astype(BF16), wo_ref[...])


def _merge(y_da, y_ssm, y_xa, proj2, col_gates, x2, w_branch, w_out):
    t, d = x2.shape
    tm = _tile(t, 512)
    assert col_gates % (3 * d) == 0
    row = lambda i: (i, 0)
    whole = lambda i: (0, 0)
    return pl.pallas_call(
        _merge_kernel,
        out_shape=jax.ShapeDtypeStruct((t, d), F32),
        grid=(t // tm,),
        in_specs=[
            pl.BlockSpec((tm, y_da.shape[1]), row),
            pl.BlockSpec((tm, y_ssm.shape[1]), row),
            pl.BlockSpec((tm, y_xa.shape[1]), row),
            pl.BlockSpec((tm, 3 * d), lambda i: (i, col_gates // (3 * d))),
            pl.BlockSpec((tm, d), row),
            pl.BlockSpec(w_branch.shape, whole),
            pl.BlockSpec(w_out.shape, whole),
        ],
        out_specs=pl.BlockSpec((tm, d), row),
        compiler_params=_cparams("parallel"),
        name="merge_out",
    )(y_da, y_ssm, y_xa, proj2, x2, w_branch, w_out)


def _ffn_kernel(x_ref, g_ref, wg_ref, wu_ref, wo_ref, fg_ref, o_ref, h_ref, acc_ref, *, final_norm):
    j = pl.program_id(1)

    @pl.when(j == 0)
    def _():
        h_ref[...] = _rms(x_ref[...], g_ref[...]).astype(BF16)
        acc_ref[...] = jnp.zeros_like(acc_ref)

    h = h_ref[...]
    gate = _dot(h, wg_ref[...])
    up = _dot(h, wu_ref[...])
    act = (gate * jax.nn.sigmoid(gate) * up).astype(BF16)
    acc_ref[...] += _dot(act, wo_ref[...])

    @pl.when(j == pl.num_programs(1) - 1)
    def _():
        y = x_ref[...] + acc_ref[...]
        if final_norm:
            y = _rms(y, fg_ref[...])
        o_ref[...] = y


def _ffn(x2, g, w_in, w_out, final_g, final_norm):
    t, d = x2.shape
    hid = w_out.shape[0]
    tm, th = _tile(t, 1024), 256
    nh = hid // th
    assert hid % th == 0
    return pl.pallas_call(
        functools.partial(_ffn_kernel, final_norm=final_norm),
        out_shape=jax.ShapeDtypeStruct((t, d), F32),
        grid=(t // tm, nh),
        in_specs=[
            pl.BlockSpec((tm, d), lambda i, j: (i, 0)),
            pl.BlockSpec((1, d), lambda i, j: (0, 0)),
            pl.BlockSpec((d, th), lambda i, j: (0, j)),
            pl.BlockSpec((d, th), lambda i, j: (0, nh + j)),
            pl.BlockSpec((th, d), lambda i, j: (j, 0)),
            pl.BlockSpec((1, d), lambda i, j: (0, 0)),
        ],
        out_specs=pl.BlockSpec((tm, d), lambda i, j: (i, 0)),
        scratch_shapes=[pltpu.VMEM((tm, d), BF16), pltpu.VMEM((tm, d), F32)],
        compiler_params=_cparams("parallel", "arbitrary"),
        name="ffn",
    )(x2, g, w_in, w_in, w_out, final_g)


def _alibi_slopes(n_heads):
    start = 2.0 ** (-8.0 / n_heads)
    return np.array([start ** (i + 1) for i in range(n_heads)], dtype=np.float32)


def kernel(x, mem, mix_norm_g, w_in, da_lambda_q1, da_lambda_k1, da_lambda_q2, da_lambda_k2, da_subln_g, ssm_conv_w, ssm_conv_b, ssm_dt_bias, ssm_A_log, ssm_D, ssm_norm_g, mem_norm_g, w_mem_kv, w_branch, w_out, ffn_norm_g, w_ffn_in, w_ffn_out, final_norm_g):
    b, s, d = x.shape
    depth = w_in.shape[0]
    t = b * s
    m = mem.shape[1]
    da_cols = DA_HEADS * 2 * DA_HEAD_DIM
    inner = SSM_HEADS * SSM_HEAD_DIM
    conv_ch = inner + 2 * SSM_GROUPS * SSM_STATE
    n_dt = 2 * SSM_HEADS
    xw = d

    o_q, o_k, o_v = 0, da_cols, 2 * da_cols
    o_z = 3 * da_cols
    o_xbc = o_z + inner
    o_dt = o_xbc + conv_ch
    o_xq = o_dt + n_dt
    o_gt = o_xq + xw
    q_scale = DA_HEAD_DIM ** -0.5 * LOG2E
    w_main = jnp.concatenate(
        [w_in[:, :, o_z:o_dt], w_in[:, :, o_q:o_k] * q_scale, w_in[:, :, o_k:o_z], w_in[:, :, o_xq:]],
        axis=2).astype(BF16)
    w_dt = jnp.pad(w_in[:, :, o_dt:o_xq], ((0, 0), (0, 0), (0, LANES - n_dt))).astype(BF16)
    c_z, c_xbc = 0, inner
    c_q = inner + conv_ch
    c_k, c_v = c_q + da_cols, c_q + 2 * da_cols
    c_xq = c_q + 3 * da_cols
    c_gt = c_xq + xw

    w_kv_b = w_mem_kv.astype(BF16)
    w_branch_b = w_branch.astype(BF16)
    w_out_b = w_out.astype(BF16)
    w_ffn_in_b = w_ffn_in.astype(BF16)
    w_ffn_out_b = w_ffn_out.astype(BF16)

    pad_lanes = lambda v: jnp.pad(v.reshape(depth, 1, n_dt), ((0, 0), (0, 0), (0, LANES - n_dt)))
    dt_bias_p = pad_lanes(ssm_dt_bias.astype(F32))
    a_log_p = pad_lanes(ssm_A_log.astype(F32))
    d_skip = jnp.repeat(ssm_D.astype(F32), SSM_HEAD_DIM, axis=1).reshape(depth, 1, inner)
    slopes = jnp.asarray(_alibi_slopes(DA_HEADS))

    kv_all = _mem_kv(mem.reshape(b * m, d), mem_norm_g.reshape(depth, 1, d), w_kv_b)

    x2 = x.reshape(t, d)
    for i in range(depth):
        lambda_init = 0.8 - 0.6 * math.exp(-0.3 * i)
        lam = (jnp.exp(jnp.sum(da_lambda_q1[i] * da_lambda_k1[i]).astype(F32))
               - jnp.exp(jnp.sum(da_lambda_q2[i] * da_lambda_k2[i]).astype(F32))
               + lambda_init).reshape(1)

        proj2, dt_raw = _in_proj(x2, mix_norm_g[i].reshape(1, d), w_main[i], w_dt[i])
        proj3 = proj2.reshape(b, s, -1)
        dt3 = dt_raw.reshape(b, s, LANES)

        xbc = _conv(proj3, ssm_conv_w[i], ssm_conv_b[i].reshape(1, conv_ch), c_xbc, conv_ch)
        y_bwd = _ssd(xbc, dt3, dt_bias_p[i], a_log_p[i], backward=True)
        y_ssm = _ssd(xbc, dt3, dt_bias_p[i], a_log_p[i], backward=False,
                     fused=(y_bwd, proj3, d_skip[i], ssm_norm_g[i].reshape(1, inner)))

        y_da = _diff_attn(proj3, slopes, lam, da_subln_g[i].reshape(1, 2 * DA_HEAD_DIM),
                          c_q, c_k, c_v, lambda_init)
        y_xa = _xattn(proj3, kv_all[i].reshape(b, m, 2 * xw), c_xq, xw)

        x2 = _merge(y_da.reshape(t, -1), y_ssm.reshape(t, inner), y_xa.reshape(t, xw),
                    proj2, c_gt, x2, w_branch_b[i], w_out_b[i])
        x2 = _ffn(x2, ffn_norm_g[i].reshape(1, d), w_ffn_in_b[i], w_ffn_out_b[i],
                  final_norm_g.reshape(1, d), final_norm=(i == depth - 1))
    return x2.reshape(b, s, d)
```

```python
import functools
import math

import numpy as np
import jax
import jax.numpy as jnp
from jax import lax
from jax.experimental import pallas as pl
from jax.experimental.pallas import tpu as pltpu

F32 = jnp.float32
BF16 = jnp.bfloat16

EPS = 1e-6
LOG2E = math.log2(math.e)
LANES = 128
MXU_DIM = 256
F32_SUBLANES = 8
BF16_SUBLANES = 16
DA_HEADS = 8
DA_HEAD_DIM = 64
DA_KEY_CHUNK = 256
SSM_HEADS = 32
SSM_HEAD_DIM = 64
SSM_GROUPS = 4
SSM_STATE = 128
SSM_CHUNK = 128
SSM_CONV = 5
CONV_COLS = 512
XA_HEADS = 4
VMEM_LIMIT = 56 * 1024 * 1024


def _cparams(*sem):
    return pltpu.CompilerParams(dimension_semantics=sem, vmem_limit_bytes=VMEM_LIMIT)


def _rms(x, g):
    ms = jnp.mean(x * x, axis=-1, keepdims=True)
    return x * lax.rsqrt(ms + EPS) * g


def _nt_dot(a, b):
    return lax.dot_general(a, b, (((1,), (1,)), ((), ())), preferred_element_type=F32)


def _dot(a, b):
    return jnp.dot(a, b, preferred_element_type=F32)


def _tile(n, want):
    t = min(n, want)
    assert n % t == 0, (n, t)
    return t


def _conv_silu(y, w_ref, b_ref, cols):
    s_len = y.shape[0]
    pad = SSM_CONV // 2
    edge = jnp.zeros((F32_SUBLANES, y.shape[1]), F32)
    yp = jnp.concatenate([edge, y, edge], axis=0)
    n = yp.shape[0]
    acc = y * w_ref[pad:pad + 1, cols] + b_ref[:, cols]
    for k in range(SSM_CONV):
        off = k - pad
        if off != 0:
            shifted = pltpu.roll(yp, (-off) % n, axis=0)[F32_SUBLANES:F32_SUBLANES + s_len]
            acc = acc + shifted * w_ref[k:k + 1, cols]
    return acc * jax.nn.sigmoid(acc)


def _in_proj_kernel(x_ref, g_ref, wh_ref, wt_ref, wdt_ref, cw_ref, cb_ref, o_ref, dt_ref, h_ref,
                    *, n_head, conv_lo, conv_hi, q_scale):
    j = pl.program_id(1)

    @pl.when(j == 0)
    def _():
        h = _rms(x_ref[...], g_ref[...]).astype(BF16)
        h_ref[...] = h
        dt_ref[...] = _dot(h, wdt_ref[...])

    is_conv = jnp.logical_and(j >= conv_lo, j < conv_hi)

    @pl.when(jnp.logical_and(j < n_head, jnp.logical_not(is_conv)))
    def _():
        scale = jnp.where(j == 0, q_scale, 1.0)
        o_ref[...] = (_dot(h_ref[...], wh_ref[...]) * scale).astype(o_ref.dtype)

    @pl.when(is_conv)
    def _():
        chunks = [slice(c, c + CONV_COLS) for c in range(0, o_ref.shape[1], CONV_COLS)]
        ys = [_dot(h_ref[...], wh_ref[:, cols]) for cols in chunks]
        for cols, y in zip(chunks, ys):
            o_ref[:, cols] = _conv_silu(y, cw_ref, cb_ref, cols).astype(o_ref.dtype)

    @pl.when(j >= n_head)
    def _():
        o_ref[...] = _dot(h_ref[...], wt_ref[...]).astype(o_ref.dtype)


def _in_proj(x2, g, w_head, w_tail, w_dt, conv_w, conv_b, layer, *, seq_len, conv_col0, q_cols, q_scale):
    t, d = x2.shape
    nh, nt = w_head.shape[2], w_tail.shape[2]
    conv_ch = conv_w.shape[2]
    tm, tn = seq_len, q_cols
    assert t % tm == 0 and nh % tn == 0 and nt % tn == 0 and conv_col0 % tn == 0 and conv_ch % tn == 0
    n_head, n_tail = nh // tn, nt // tn
    conv_lo, conv_hi = conv_col0 // tn, (conv_col0 + conv_ch) // tn
    assert conv_hi <= n_head
    conv_blk = lambda j: jnp.clip(j - conv_lo, 0, conv_hi - conv_lo - 1)
    return pl.pallas_call(
        functools.partial(_in_proj_kernel, n_head=n_head, conv_lo=conv_lo, conv_hi=conv_hi, q_scale=q_scale),
        out_shape=(jax.ShapeDtypeStruct((t, nh + nt), BF16), jax.ShapeDtypeStruct((t, LANES), F32)),
        grid=(t // tm, n_head + n_tail),
        in_specs=[
            pl.BlockSpec((tm, d), lambda i, j: (i, 0), pipeline_mode=pl.Buffered(1)),
            pl.BlockSpec((1, d), lambda i, j: (0, 0)),
            pl.BlockSpec((None, d, tn), lambda i, j: (layer, 0, jnp.minimum(j, n_head - 1))),
            pl.BlockSpec((None, d, tn), lambda i, j: (layer, 0, jnp.maximum(j - n_head, 0))),
            pl.BlockSpec((None, d, LANES), lambda i, j: (layer, 0, 0)),
            pl.BlockSpec((None, SSM_CONV, tn), lambda i, j: (layer, 0, conv_blk(j))),
            pl.BlockSpec((None, 1, tn), lambda i, j: (layer, 0, conv_blk(j))),
        ],
        out_specs=(
            pl.BlockSpec((tm, tn), lambda i, j: (i, j)),
            pl.BlockSpec((tm, LANES), lambda i, j: (i, 0)),
        ),
        scratch_shapes=[pltpu.VMEM((tm, d), BF16)],
        compiler_params=_cparams("parallel", "arbitrary"),
        name="in_proj",
    )(x2, g, w_head, w_tail, w_dt, conv_w, conv_b)


def _softplus(x):
    return jnp.maximum(x, 0.0) + jnp.log(1.0 + jnp.exp(-jnp.abs(x)))


def _ssd_kernel(*refs, backward, fuse):
    if fuse:
        xs0, xs1, b_ref, c_ref, dt_ref, dtb_ref, alog_ref, yb_ref, z0, z1, dsk_ref, ng_ref, o_ref, st_ref = refs
    else:
        xs0, xs1, b_ref, c_ref, dt_ref, dtb_ref, alog_ref, o_ref, st_ref = refs
    q = SSM_CHUNK

    def cols(halves, sl):
        hw = halves[0].shape[2]
        k = sl.start // hw
        return halves[k][0, :, sl.start - k * hw:sl.stop - k * hw]

    @pl.when(pl.program_id(1) == 0)
    def _():
        st_ref[...] = jnp.zeros_like(st_ref)

    dt = _softplus(dt_ref[0] + dtb_ref[...])
    a = dt * (-LOG2E * jnp.exp(alog_ref[...]))
    row = lax.broadcasted_iota(jnp.int32, (q, LANES), 0)
    cs = a
    k = 1
    while k < q:
        if backward:
            cs = cs + jnp.where(row < q - k, pltpu.roll(cs, q - k, axis=0), 0.0)
        else:
            cs = cs + jnp.where(row >= k, pltpu.roll(cs, k, axis=0), 0.0)
        k *= 2
    cs_t = cs.T
    dt_t = dt.T
    end = 0 if backward else q - 1
    w_state_t = dt_t * jnp.exp2(cs_t[:, end:end + 1] - cs_t)

    li = lax.broadcasted_iota(jnp.int32, (q, q), 0)
    si = lax.broadcasted_iota(jnp.int32, (q, q), 1)
    tri = (li <= si) if backward else (li >= si)
    lo = lax.broadcasted_iota(jnp.int32, (q, LANES), 1) < SSM_HEAD_DIM
    lane0 = SSM_HEADS if backward else 0
    pairs_per_group = SSM_HEADS // SSM_GROUPS // 2
    gw = SSM_HEADS // SSM_GROUPS * SSM_HEAD_DIM

    def blockdiag(v):
        zero = jnp.zeros_like(v)
        return jnp.concatenate([jnp.where(lo, v, zero), jnp.where(lo, zero, v)], axis=0)

    src_t = cs_t - jnp.log2(dt_t)

    for g in range(SSM_GROUPS):
        gs = slice(g * gw, (g + 1) * gw)
        bg = b_ref[0, :, g * SSM_STATE:(g + 1) * SSM_STATE]
        cg = c_ref[0, :, g * SSM_STATE:(g + 1) * SSM_STATE]
        cb = _nt_dot(cg, bg)
        bg_t = bg.astype(F32).T
        y_in = _dot(cg, st_ref[:, gs].astype(BF16))
        ys = []
        for pp in range(pairs_per_group):
            p = g * pairs_per_group + pp
            sl = slice(p * LANES, (p + 1) * LANES)
            xbd = blockdiag(cols((xs0, xs1), sl))
            m_l, b_l, e_in = [], [], []
            for h in (2 * p, 2 * p + 1):
                r = lane0 + h
                col = jnp.broadcast_to(cs[:, r:r + 1], (q, q))
                m_l.append((cb * jnp.exp2(jnp.where(tri, col - src_t[r:r + 1, :], -jnp.inf))).astype(BF16))
                b_l.append((bg_t * w_state_t[r:r + 1, :]).astype(BF16))
                e_in.append(jnp.exp2(col))
            e_pair = jnp.where(lo, e_in[0], e_in[1])
            y = _dot(jnp.concatenate(m_l, axis=1), xbd) + y_in[:, pp * LANES:(pp + 1) * LANES] * e_pair
            st_new = _dot(jnp.concatenate(b_l, axis=1), xbd)
            st_ref[:, sl] = st_ref[:, sl] * e_pair[end:end + 1, :] + st_new
            if fuse:
                ys.append(y)
            else:
                o_ref[0, :, sl] = y
        if fuse:
            yg = (jnp.concatenate(ys, axis=1) + yb_ref[0, :, gs]
                  + cols((xs0, xs1), gs).astype(F32) * dsk_ref[:, gs])
            zg = cols((z0, z1), gs).astype(F32)
            yg = yg * (zg * jax.nn.sigmoid(zg))
            o_ref[0, :, gs] = _rms(yg, ng_ref[:, gs]).astype(o_ref.dtype)


def _ssd(proj3, col_xbc, dt_raw, dt_bias, a_log, *, backward, fused=None):
    b, s, _ = proj3.shape
    q = SSM_CHUNK
    nc = s // q
    inner = SSM_HEADS * SSM_HEAD_DIM
    hw = inner // 2
    bc = SSM_GROUPS * SSM_STATE
    assert col_xbc % hw == 0 and (col_xbc + inner) % bc == 0
    cix = (lambda c: nc - 1 - c) if backward else (lambda c: c)

    def halves(col):
        return [pl.BlockSpec((1, q, hw), lambda i, c, k=k: (i, cix(c), col // hw + k)) for k in range(2)]

    in_specs = halves(col_xbc) + [
        pl.BlockSpec((1, q, bc), lambda i, c: (i, cix(c), (col_xbc + inner) // bc)),
        pl.BlockSpec((1, q, bc), lambda i, c: (i, cix(c), (col_xbc + inner) // bc + 1)),
        pl.BlockSpec((1, q, LANES), lambda i, c: (i, cix(c), 0)),
        pl.BlockSpec((1, LANES), lambda i, c: (0, 0)),
        pl.BlockSpec((1, LANES), lambda i, c: (0, 0)),
    ]
    args = [proj3, proj3, proj3, proj3, dt_raw, dt_bias, a_log]
    if fused is not None:
        y_bwd, col_z, d_skip, norm_g = fused
        assert col_z % hw == 0
        in_specs += [pl.BlockSpec((1, q, inner), lambda i, c: (i, cix(c), 0))] + halves(col_z) + [
            pl.BlockSpec((1, inner), lambda i, c: (0, 0)),
            pl.BlockSpec((1, inner), lambda i, c: (0, 0)),
        ]
        args += [y_bwd, proj3, proj3, d_skip, norm_g]
    return pl.pallas_call(
        functools.partial(_ssd_kernel, backward=backward, fuse=fused is not None),
        out_shape=jax.ShapeDtypeStruct((b, s, inner), BF16 if fused is not None else F32),
        grid=(b, nc),
        in_specs=in_specs,
        out_specs=pl.BlockSpec((1, q, inner), lambda i, c: (i, cix(c), 0)),
        scratch_shapes=[pltpu.VMEM((SSM_STATE, inner), F32)],
        compiler_params=_cparams("parallel", "arbitrary"),
        name="ssd_bwd" if backward else "ssd_fwd",
    )(*args)


def _diff_attn_kernel(slopes_ref, lam_ref, q_ref, k_ref, v_ref, g_ref, o_ref, bias_ref, vext_ref, s_ref, m_ref, a_ref,
                      *, out_scale):
    tq = q_ref.shape[1] // 2
    s_len = k_ref.shape[1]
    w = v_ref.shape[2]
    npb = s_len // (2 * tq)
    h, g = pl.program_id(0), pl.program_id(1)
    n_pairs = pl.num_programs(1) - 1
    ga = jnp.minimum(g, n_pairs - 1)
    ba, pa = ga // npb, ga % npb
    bb = jnp.maximum(g - 1, 0) // npb

    @pl.when(g == 0)
    def _():
        s_ref[...] = jnp.zeros_like(s_ref)
        m_ref[...] = jnp.zeros_like(m_ref)
        a_ref[...] = jnp.ones_like(a_ref)

    @pl.when(jnp.logical_and(ba == 0, g < n_pairs))
    def _():
        kpos = lax.broadcasted_iota(jnp.int32, (s_len, tq), 0)
        for half in range(2):
            qpos = (2 * pa + half) * tq + lax.broadcasted_iota(jnp.int32, (s_len, tq), 1)
            bias_ref[2 * pa + half] = (slopes_ref[h] * LOG2E) * jnp.abs(qpos - kpos).astype(F32)

    @pl.when(jnp.logical_and(pa == 0, g < n_pairs))
    def _():
        vext_ref[ba % 2, :w, :] = v_ref[0].astype(F32).T.astype(BF16)
        vext_ref[ba % 2, w:, :] = jnp.ones((vext_ref.shape[1] - w, s_len), BF16)

    def raw_scores(half):
        q = q_ref[0, half * tq:(half + 1) * tq, :]
        lo = lax.broadcasted_iota(jnp.int32, q.shape, 1) < DA_HEAD_DIM
        zero = jnp.zeros_like(q)
        return _nt_dot(k_ref[0], jnp.concatenate([jnp.where(lo, q, zero), jnp.where(lo, zero, q)], axis=0))

    def biased(raw, half):
        bias = bias_ref[2 * pa + half]
        s = jnp.concatenate([raw[:, :tq] - bias, raw[:, tq:] - bias], axis=1)
        return s, jnp.max(s, axis=0, keepdims=True)

    def epilogue(a):
        a1, a2 = a[:, :tq], a[:, tq:]
        o_t = a1[:w] * (1.0 / a1[w:w + 1]) - a2[:w] * (lam_ref[0] / a2[w:w + 1])
        return (_rms(o_t.T, g_ref[...]) * out_scale).astype(o_ref.dtype)

    raw0 = raw_scores(0)
    p_prev = jnp.exp2(s_ref[...] - m_ref[...]).astype(BF16)
    o_ref[0, :tq, :] = epilogue(a_ref[...])
    a_prev = _dot(vext_ref[bb % 2], p_prev)
    s0, m0 = biased(raw0, 0)
    raw1 = raw_scores(1)
    p0 = jnp.exp2(s0 - m0).astype(BF16)
    o_ref[0, tq:, :] = epilogue(a_prev)
    a_ref[...] = _dot(vext_ref[ba % 2], p0)
    s1, m1 = biased(raw1, 1)
    s_ref[...] = s1
    m_ref[...] = m1


def _diff_attn(proj3, slopes, lam, sub_g, col_q, col_k, col_v, lambda_init):
    b, s, _ = proj3.shape
    w = 2 * DA_HEAD_DIM
    tq = _tile(s, 512) // 2
    npb = s // (2 * tq)
    n_pairs = b * npb
    smem = pl.BlockSpec(memory_space=pltpu.SMEM)

    def pair_in(g):
        ga = jnp.minimum(g, n_pairs - 1)
        return ga // npb, ga % npb

    def pair_out(g):
        gb = jnp.maximum(g - 1, 0)
        return gb // npb, gb % npb

    return pl.pallas_call(
        functools.partial(_diff_attn_kernel, out_scale=1.0 - lambda_init),
        out_shape=jax.ShapeDtypeStruct((b, s, DA_HEADS * w), BF16),
        grid=(DA_HEADS, n_pairs + 1),
        in_specs=[
            smem, smem,
            pl.BlockSpec((1, 2 * tq, w), lambda h, g: (*pair_in(g), col_q // w + h)),
            pl.BlockSpec((1, s, w), lambda h, g: (pair_in(g)[0], 0, col_k // w + h)),
            pl.BlockSpec((1, s, w), lambda h, g: (pair_in(g)[0], 0, col_v // w + h)),
            pl.BlockSpec((1, w), lambda h, g: (0, 0)),
        ],
        out_specs=pl.BlockSpec((1, 2 * tq, w), lambda h, g: (*pair_out(g), h)),
        scratch_shapes=[
            pltpu.VMEM((s // tq, s, tq), F32),
            pltpu.VMEM((2, w + BF16_SUBLANES, s), BF16),
            pltpu.VMEM((s, 2 * tq), F32),
            pltpu.VMEM((1, 2 * tq), F32),
            pltpu.VMEM((w + BF16_SUBLANES, 2 * tq), F32),
        ],
        compiler_params=_cparams("parallel", "arbitrary"),
        name="diff_attn",
    )(slopes, lam, proj3, proj3, proj3, sub_g)


def _mem_kv_kernel(m_ref, g_ref, w_ref, o_ref):
    mn = _rms(m_ref[...], g_ref[0]).astype(BF16)
    o_ref[0] = _dot(mn, w_ref[0]).astype(o_ref.dtype)


def _mem_kv(mem2, mem_g, w_kv):
    rows, d = mem2.shape
    depth, _, n = w_kv.shape
    tn = _tile(n, 1024)
    return pl.pallas_call(
        _mem_kv_kernel,
        out_shape=jax.ShapeDtypeStruct((depth, rows, n), BF16),
        grid=(depth, n // tn),
        in_specs=[
            pl.BlockSpec((rows, d), lambda l, j: (0, 0)),
            pl.BlockSpec((1, 1, d), lambda l, j: (l, 0, 0)),
            pl.BlockSpec((1, d, tn), lambda l, j: (l, 0, j)),
        ],
        out_specs=pl.BlockSpec((1, rows, tn), lambda l, j: (l, 0, j)),
        compiler_params=_cparams("parallel", "parallel"),
        name="mem_kv",
    )(mem2, mem_g, w_kv)


def _xattn_kernel(q_ref, kv_ref, o_ref):
    xw = q_ref.shape[2]
    hd = xw // XA_HEADS
    for h in range(XA_HEADS):
        sl = slice(h * hd, (h + 1) * hd)
        qh = q_ref[0, :, sl] * (hd ** -0.5)
        s = _nt_dot(qh, kv_ref[0, :, sl])
        p = jnp.exp(s - jnp.max(s, axis=-1, keepdims=True))
        p = p * (1.0 / jnp.sum(p, axis=-1, keepdims=True))
        o_ref[0, :, sl] = _dot(p.astype(BF16), kv_ref[0, :, xw + h * hd:xw + (h + 1) * hd]).astype(o_ref.dtype)


def _xattn(proj3, kv, col_q, xw, layer):
    b, s, _ = proj3.shape
    m = kv.shape[1] // b
    tq = _tile(s, 512)
    return pl.pallas_call(
        _xattn_kernel,
        out_shape=jax.ShapeDtypeStruct((b, s, xw), BF16),
        grid=(b, s // tq),
        in_specs=[
            pl.BlockSpec((1, tq, xw), lambda i, t: (i, t, col_q // xw)),
            pl.BlockSpec((1, m, 2 * xw), lambda i, t: (layer, i, 0)),
        ],
        out_specs=pl.BlockSpec((1, tq, xw), lambda i, t: (i, t, 0)),
        compiler_params=_cparams("parallel", "parallel"),
        name="mem_xattn",
    )(proj3, kv)


def _merge_kernel(yda_ref, yssm_ref, yxa_ref, gt_ref, x_ref, wb_ref, wo_ref, o_ref):
    d = x_ref.shape[1]
    n_da, n_ssm = yda_ref.shape[1], yssm_ref.shape[1]
    gate = jax.nn.sigmoid(gt_ref[...].astype(F32))
    merged = (gate[:, :d] * _dot(yda_ref[...], wb_ref[:n_da, :])
              + gate[:, d:2 * d] * _dot(yssm_ref[...], wb_ref[n_da:n_da + n_ssm, :])
              + gate[:, 2 * d:] * _dot(yxa_ref[...], wb_ref[n_da + n_ssm:, :]))
    o_ref[...] = x_ref[...] + _dot(merged.astype(BF16), wo_ref[...])


def _merge(y_da, y_ssm, y_xa, proj2, col_gates, x2, w_branch, w_out, layer):
    t, d = x2.shape
    tm = _tile(t, 512)
    assert col_gates % (3 * d) == 0
    row = lambda i: (i, 0)
    whole = lambda i: (layer, 0, 0)
    return pl.pallas_call(
        _merge_kernel,
        out_shape=jax.ShapeDtypeStruct((t, d), F32),
        grid=(t // tm,),
        in_specs=[
            pl.BlockSpec((tm, y_da.shape[1]), row),
            pl.BlockSpec((tm, y_ssm.shape[1]), row),
            pl.BlockSpec((tm, y_xa.shape[1]), row),
            pl.BlockSpec((tm, 3 * d), lambda i: (i, col_gates // (3 * d))),
            pl.BlockSpec((tm, d), row),
            pl.BlockSpec((None,) + w_branch.shape[1:], whole),
            pl.BlockSpec((None,) + w_out.shape[1:], whole),
        ],
        out_specs=pl.BlockSpec((tm, d), row),
        compiler_params=_cparams("parallel"),
        name="merge_out",
    )(y_da, y_ssm, y_xa, proj2, x2, w_branch, w_out)


def _ffn_kernel(x_ref, g_ref, wi_ref, wo_ref, fg_ref, o_ref, *, final_norm, chunks):
    hid = wo_ref.shape[0]
    x = x_ref[...]
    h = _rms(x, g_ref[...]).astype(BF16)
    y = x
    for c0, c1 in chunks:
        gate = _dot(h, wi_ref[:, c0:c1])
        up = _dot(h, wi_ref[:, hid + c0:hid + c1])
        act = (gate * jax.nn.sigmoid(gate) * up).astype(BF16)
        y = y + _dot(act, wo_ref[c0:c1, :])
    if final_norm:
        y = _rms(y, fg_ref[...])
    o_ref[...] = y


def _ffn(x2, g, w_in, w_out, final_g, final_norm, layer):
    t, d = x2.shape
    hid = w_out.shape[1]
    tm = _tile(t, 512)
    assert hid % MXU_DIM == 0
    step = 3 * MXU_DIM
    chunks = tuple((c, min(c + step, hid)) for c in range(0, hid, step))
    resident = pl.Buffered(1)
    return pl.pallas_call(
        functools.partial(_ffn_kernel, final_norm=final_norm, chunks=chunks),
        out_shape=jax.ShapeDtypeStruct((t, d), F32),
        grid=(t // tm,),
        in_specs=[
            pl.BlockSpec((tm, d), lambda i: (i, 0)),
            pl.BlockSpec((1, d), lambda i: (0, 0)),
            pl.BlockSpec((None,) + w_in.shape[1:], lambda i: (layer, 0, 0), pipeline_mode=resident),
            pl.BlockSpec((None,) + w_out.shape[1:], lambda i: (layer, 0, 0), pipeline_mode=resident),
            pl.BlockSpec((1, d), lambda i: (0, 0)),
        ],
        out_specs=pl.BlockSpec((tm, d), lambda i: (i, 0)),
        compiler_params=_cparams("parallel"),
        name="ffn",
    )(x2, g, w_in, w_out, final_g)


def _alibi_slopes(n_heads):
    start = 2.0 ** (-8.0 / n_heads)
    return np.array([start ** (i + 1) for i in range(n_heads)], dtype=np.float32)


def kernel(x, mem, mix_norm_g, w_in, da_lambda_q1, da_lambda_k1, da_lambda_q2, da_lambda_k2, da_subln_g, ssm_conv_w, ssm_conv_b, ssm_dt_bias, ssm_A_log, ssm_D, ssm_norm_g, mem_norm_g, w_mem_kv, w_branch, w_out, ffn_norm_g, w_ffn_in, w_ffn_out, final_norm_g):
    b, s, d = x.shape
    depth = w_in.shape[0]
    t = b * s
    m = mem.shape[1]
    da_cols = DA_HEADS * 2 * DA_HEAD_DIM
    inner = SSM_HEADS * SSM_HEAD_DIM
    conv_ch = inner + 2 * SSM_GROUPS * SSM_STATE
    n_dt = 2 * SSM_HEADS
    xw = d

    o_q, o_k, o_v = 0, da_cols, 2 * da_cols
    o_z = 3 * da_cols
    o_xbc = o_z + inner
    o_dt = o_xbc + conv_ch
    o_xq = o_dt + n_dt
    o_gt = o_xq + xw
    assert o_dt % LANES == 0 and n_dt <= LANES
    w_head = w_in[:, :, :o_dt].astype(BF16)
    w_tail = w_in[:, :, o_xq:].astype(BF16)
    w_dt = w_in[:, :, o_dt:o_dt + LANES].astype(BF16)
    c_q, c_k, c_v, c_z, c_xbc = o_q, o_k, o_v, o_z, o_xbc
    c_xq = o_dt
    c_gt = c_xq + xw
    q_scale = DA_HEAD_DIM ** -0.5 * LOG2E

    w_kv_b = w_mem_kv.astype(BF16)
    w_branch_b = w_branch.astype(BF16)
    w_out_b = w_out.astype(BF16)
    w_ffn_in_b = w_ffn_in.astype(BF16)
    w_ffn_out_b = w_ffn_out.astype(BF16)

    pad_lanes = lambda v: jnp.pad(v.reshape(depth, 1, n_dt), ((0, 0), (0, 0), (0, LANES - n_dt)))
    dt_bias_p = pad_lanes(ssm_dt_bias.astype(F32))
    a_log_p = pad_lanes(ssm_A_log.astype(F32))
    d_skip = jnp.repeat(ssm_D.astype(F32), SSM_HEAD_DIM, axis=1).reshape(depth, 1, inner)
    slopes = jnp.asarray(_alibi_slopes(DA_HEADS))
    conv_b3 = ssm_conv_b.reshape(depth, 1, conv_ch)

    kv_all = _mem_kv(mem.reshape(b * m, d), mem_norm_g.reshape(depth, 1, d), w_kv_b)

    x2 = x.reshape(t, d)
    for i in range(depth):
        lambda_init = 0.8 - 0.6 * math.exp(-0.3 * i)
        lam = (jnp.exp(jnp.sum(da_lambda_q1[i] * da_lambda_k1[i]).astype(F32))
               - jnp.exp(jnp.sum(da_lambda_q2[i] * da_lambda_k2[i]).astype(F32))
               + lambda_init).reshape(1)

        proj2, dt_raw = _in_proj(x2, mix_norm_g[i].reshape(1, d), w_head, w_tail, w_dt, ssm_conv_w, conv_b3, i,
                                 seq_len=s, conv_col0=c_xbc, q_cols=da_cols, q_scale=q_scale)
        proj3 = proj2.reshape(b, s, -1)
        dt3 = dt_raw.reshape(b, s, LANES)

        y_bwd = _ssd(proj3, c_xbc, dt3, dt_bias_p[i], a_log_p[i], backward=True)
        y_ssm = _ssd(proj3, c_xbc, dt3, dt_bias_p[i], a_log_p[i], backward=False,
                     fused=(y_bwd, c_z, d_skip[i], ssm_norm_g[i].reshape(1, inner)))

        y_da = _diff_attn(proj3, slopes, lam, da_subln_g[i].reshape(1, 2 * DA_HEAD_DIM),
                          c_q, c_k, c_v, lambda_init)
        y_xa = _xattn(proj3, kv_all, c_xq, xw, i)

        x2 = _merge(y_da.reshape(t, -1), y_ssm.reshape(t, inner), y_xa.reshape(t, xw),
                    proj2, c_gt, x2, w_branch_b, w_out_b, i)
        x2 = _ffn(x2, ffn_norm_g[i].reshape(1, d), w_ffn_in_b, w_ffn_out_b,
                  final_norm_g.reshape(1, d), final_norm=(i == depth - 1), layer=i)
    return x2.reshape(b, s, d)
```

```python
import functools
import math

import numpy as np
import jax
import jax.numpy as jnp
from jax import lax
from jax.experimental import pallas as pl
from jax.experimental.pallas import tpu as pltpu

F32 = jnp.float32
BF16 = jnp.bfloat16

EPS = 1e-6
LOG2E = math.log2(math.e)
LANES = 128
MXU_DIM = 256
F32_SUBLANES = 8
BF16_SUBLANES = 16
DA_HEADS = 8
DA_HEAD_DIM = 64
DA_KEY_CHUNK = 256
SSM_HEADS = 32
SSM_HEAD_DIM = 64
SSM_GROUPS = 4
SSM_STATE = 128
SSM_CHUNK = 128
SSM_CONV = 5
IN_PROJ_STEPS = 12
XA_HEADS = 4
VMEM_LIMIT = 56 * 1024 * 1024


def _cparams(*sem):
    return pltpu.CompilerParams(dimension_semantics=sem, vmem_limit_bytes=VMEM_LIMIT)


def _rms(x, g):
    ms = jnp.mean(x * x, axis=-1, keepdims=True)
    return x * lax.rsqrt(ms + EPS) * g


def _nt_dot(a, b):
    return lax.dot_general(a, b, (((1,), (1,)), ((), ())), preferred_element_type=F32)


def _dot(a, b):
    return jnp.dot(a, b, preferred_element_type=F32)


def _tile(n, want):
    t = min(n, want)
    assert n % t == 0, (n, t)
    return t


def _conv_silu(y, w_ref, b_ref, cols):
    s_len = y.shape[0]
    pad = SSM_CONV // 2
    assert pad <= F32_SUBLANES
    edge = jnp.zeros((F32_SUBLANES, y.shape[1]), F32)
    yp = jnp.concatenate([edge, y, edge], axis=0)
    n = yp.shape[0]
    w = lambda k: w_ref[k:k + 1, cols]
    later = earlier = None
    for d in range(pad, 0, -1):
        nxt, prv = yp * w(pad + d), yp * w(pad - d)
        later = pltpu.roll(nxt if later is None else later + nxt, n - 1, axis=0)
        earlier = pltpu.roll(prv if earlier is None else earlier + prv, 1, axis=0)
    acc = y * w(pad) + b_ref[:, cols] + (later + earlier)[F32_SUBLANES:F32_SUBLANES + s_len]
    return acc * jax.nn.sigmoid(acc)


def _in_proj_kernel(x_ref, g_ref, wp_ref, wc_ref, wdt_ref, cw_ref, cb_ref, o_ref, oc_ref, dt_ref, h_ref):
    @pl.when(pl.program_id(1) == 0)
    def _():
        h = _rms(x_ref[...], g_ref[...]).astype(BF16)
        h_ref[...] = h
        dt_ref[...] = _dot(h, wdt_ref[...])

    h = h_ref[...]
    y = _dot(h, wc_ref[...])
    oc_ref[...] = _conv_silu(y, cw_ref, cb_ref, slice(None)).astype(oc_ref.dtype)
    o_ref[...] = _dot(h, wp_ref[...]).astype(o_ref.dtype)


def _in_proj(x2, g, w_plain, w_conv, w_all, dt_col, conv_w, conv_b, layer, *, seq_len, steps):
    t, d = x2.shape
    n_plain, n_conv = w_plain.shape[2], w_conv.shape[2]
    tm = seq_len
    tp, tc = n_plain // steps, n_conv // steps
    assert t % tm == 0 and n_plain % steps == 0 and n_conv % steps == 0 and tp % LANES == 0 and tc % LANES == 0
    assert dt_col % LANES == 0
    return pl.pallas_call(
        _in_proj_kernel,
        out_shape=(jax.ShapeDtypeStruct((t, n_plain), BF16), jax.ShapeDtypeStruct((t, n_conv), BF16),
                   jax.ShapeDtypeStruct((t, LANES), F32)),
        grid=(t // tm, steps),
        in_specs=[
            pl.BlockSpec((tm, d), lambda i, j: (i, 0), pipeline_mode=pl.Buffered(1)),
            pl.BlockSpec((1, d), lambda i, j: (0, 0)),
            pl.BlockSpec((None, d, tp), lambda i, j: (layer, 0, j)),
            pl.BlockSpec((None, d, tc), lambda i, j: (layer, 0, j)),
            pl.BlockSpec((None, d, LANES), lambda i, j: (layer, 0, dt_col // LANES)),
            pl.BlockSpec((None, SSM_CONV, tc), lambda i, j: (layer, 0, j)),
            pl.BlockSpec((None, 1, tc), lambda i, j: (layer, 0, j)),
        ],
        out_specs=(
            pl.BlockSpec((tm, tp), lambda i, j: (i, j)),
            pl.BlockSpec((tm, tc), lambda i, j: (i, j)),
            pl.BlockSpec((tm, LANES), lambda i, j: (i, 0)),
        ),
        scratch_shapes=[pltpu.VMEM((tm, d), BF16)],
        compiler_params=_cparams("parallel", "arbitrary"),
        name="in_proj",
    )(x2, g, w_plain, w_conv, w_all, conv_w, conv_b)


def _softplus(x):
    return jnp.maximum(x, 0.0) + jnp.log(1.0 + jnp.exp(-jnp.abs(x)))


def _ssd_kernel(*refs, backward, fuse):
    if fuse:
        xs0, xs1, b_ref, c_ref, dt_ref, dtb_ref, alog_ref, yb_ref, z0, z1, dsk_ref, ng_ref, o_ref, st_ref = refs
    else:
        xs0, xs1, b_ref, c_ref, dt_ref, dtb_ref, alog_ref, o_ref, st_ref = refs
    q = SSM_CHUNK

    def cols(halves, sl):
        hw = halves[0].shape[2]
        k = sl.start // hw
        return halves[k][0, :, sl.start - k * hw:sl.stop - k * hw]

    @pl.when(pl.program_id(1) == 0)
    def _():
        st_ref[...] = jnp.zeros_like(st_ref)

    dt = _softplus(dt_ref[0] + dtb_ref[...])
    a = dt * (-LOG2E * jnp.exp(alog_ref[...]))
    row = lax.broadcasted_iota(jnp.int32, (q, LANES), 0)
    cs = a
    k = 1
    while k < q:
        if backward:
            cs = cs + jnp.where(row < q - k, pltpu.roll(cs, q - k, axis=0), 0.0)
        else:
            cs = cs + jnp.where(row >= k, pltpu.roll(cs, k, axis=0), 0.0)
        k *= 2
    cs_t = cs.T
    dt_t = dt.T
    end = 0 if backward else q - 1
    w_state_t = dt_t * jnp.exp2(cs_t[:, end:end + 1] - cs_t)

    li = lax.broadcasted_iota(jnp.int32, (q, q), 0)
    si = lax.broadcasted_iota(jnp.int32, (q, q), 1)
    tri = (li <= si) if backward else (li >= si)
    lo = lax.broadcasted_iota(jnp.int32, (q, LANES), 1) < SSM_HEAD_DIM
    lane0 = SSM_HEADS if backward else 0
    pairs_per_group = SSM_HEADS // SSM_GROUPS // 2
    gw = SSM_HEADS // SSM_GROUPS * SSM_HEAD_DIM

    def blockdiag(v):
        zero = jnp.zeros_like(v)
        return jnp.concatenate([jnp.where(lo, v, zero), jnp.where(lo, zero, v)], axis=0)

    src_t = cs_t - jnp.log2(dt_t)

    for g in range(SSM_GROUPS):
        gs = slice(g * gw, (g + 1) * gw)
        bg = b_ref[0, :, g * SSM_STATE:(g + 1) * SSM_STATE]
        cg = c_ref[0, :, g * SSM_STATE:(g + 1) * SSM_STATE]
        cb = _nt_dot(cg, bg)
        bg_t = bg.astype(F32).T
        y_in = _dot(cg, st_ref[:, gs].astype(BF16))
        ys = []
        for pp in range(pairs_per_group):
            p = g * pairs_per_group + pp
            sl = slice(p * LANES, (p + 1) * LANES)
            xbd = blockdiag(cols((xs0, xs1), sl))
            m_l, b_l, e_in = [], [], []
            for h in (2 * p, 2 * p + 1):
                r = lane0 + h
                col = jnp.broadcast_to(cs[:, r:r + 1], (q, q))
                m_l.append((cb * jnp.exp2(jnp.where(tri, col - src_t[r:r + 1, :], -jnp.inf))).astype(BF16))
                b_l.append((bg_t * w_state_t[r:r + 1, :]).astype(BF16))
                e_in.append(jnp.exp2(col))
            e_pair = jnp.where(lo, e_in[0], e_in[1])
            y = _dot(jnp.concatenate(m_l, axis=1), xbd) + y_in[:, pp * LANES:(pp + 1) * LANES] * e_pair
            st_new = _dot(jnp.concatenate(b_l, axis=1), xbd)
            st_ref[:, sl] = st_ref[:, sl] * e_pair[end:end + 1, :] + st_new
            if fuse:
                ys.append(y)
            else:
                o_ref[0, :, sl] = y
        if fuse:
            yg = (jnp.concatenate(ys, axis=1) + yb_ref[0, :, gs]
                  + cols((xs0, xs1), gs).astype(F32) * dsk_ref[:, gs])
            zg = cols((z0, z1), gs).astype(F32)
            yg = yg * (zg * jax.nn.sigmoid(zg))
            o_ref[0, :, gs] = _rms(yg, ng_ref[:, gs]).astype(o_ref.dtype)


def _ssd(xbc, col_xbc, dt_raw, dt_bias, a_log, *, backward, fused=None):
    b, s, _ = xbc.shape
    q = SSM_CHUNK
    nc = s // q
    inner = SSM_HEADS * SSM_HEAD_DIM
    hw = inner // 2
    bc = SSM_GROUPS * SSM_STATE
    assert col_xbc % hw == 0 and (col_xbc + inner) % bc == 0
    cix = (lambda c: nc - 1 - c) if backward else (lambda c: c)

    def halves(col):
        return [pl.BlockSpec((1, q, hw), lambda i, c, k=k: (i, cix(c), col // hw + k)) for k in range(2)]

    in_specs = halves(col_xbc) + [
        pl.BlockSpec((1, q, bc), lambda i, c: (i, cix(c), (col_xbc + inner) // bc)),
        pl.BlockSpec((1, q, bc), lambda i, c: (i, cix(c), (col_xbc + inner) // bc + 1)),
        pl.BlockSpec((1, q, LANES), lambda i, c: (i, cix(c), 0)),
        pl.BlockSpec((1, LANES), lambda i, c: (0, 0)),
        pl.BlockSpec((1, LANES), lambda i, c: (0, 0)),
    ]
    args = [xbc, xbc, xbc, xbc, dt_raw, dt_bias, a_log]
    if fused is not None:
        y_bwd, z_src, col_z, d_skip, norm_g = fused
        assert col_z % hw == 0
        in_specs += [pl.BlockSpec((1, q, inner), lambda i, c: (i, cix(c), 0))] + halves(col_z) + [
            pl.BlockSpec((1, inner), lambda i, c: (0, 0)),
            pl.BlockSpec((1, inner), lambda i, c: (0, 0)),
        ]
        args += [y_bwd, z_src, z_src, d_skip, norm_g]
    return pl.pallas_call(
        functools.partial(_ssd_kernel, backward=backward, fuse=fused is not None),
        out_shape=jax.ShapeDtypeStruct((b, s, inner), BF16 if fused is not None else F32),
        grid=(b, nc),
        in_specs=in_specs,
        out_specs=pl.BlockSpec((1, q, inner), lambda i, c: (i, cix(c), 0)),
        scratch_shapes=[pltpu.VMEM((SSM_STATE, inner), F32)],
        compiler_params=_cparams("parallel", "arbitrary"),
        name="ssd_bwd" if backward else "ssd_fwd",
    )(*args)


def _diff_attn_kernel(slopes_ref, lam_ref, q_ref, k_ref, v_ref, g_ref, o_ref, bias_ref, vext_ref, s_ref, m_ref, a_ref,
                      *, out_scale):
    tq = q_ref.shape[1] // 2
    s_len = k_ref.shape[1]
    w = v_ref.shape[2]
    npb = s_len // (2 * tq)
    h, g = pl.program_id(0), pl.program_id(1)
    n_pairs = pl.num_programs(1) - 1
    ga = jnp.minimum(g, n_pairs - 1)
    ba, pa = ga // npb, ga % npb
    bb = jnp.maximum(g - 1, 0) // npb

    @pl.when(g == 0)
    def _():
        s_ref[...] = jnp.zeros_like(s_ref)
        m_ref[...] = jnp.zeros_like(m_ref)
        a_ref[...] = jnp.ones_like(a_ref)

    @pl.when(jnp.logical_and(ba == 0, g < n_pairs))
    def _():
        kpos = lax.broadcasted_iota(jnp.int32, (s_len, tq), 0)
        for half in range(2):
            qpos = (2 * pa + half) * tq + lax.broadcasted_iota(jnp.int32, (s_len, tq), 1)
            bias_ref[2 * pa + half] = (slopes_ref[h] * LOG2E) * jnp.abs(qpos - kpos).astype(F32)

    @pl.when(jnp.logical_and(pa == 0, g < n_pairs))
    def _():
        vext_ref[ba % 2, :w, :] = v_ref[0].astype(F32).T.astype(BF16)
        vext_ref[ba % 2, w:, :] = jnp.ones((vext_ref.shape[1] - w, s_len), BF16)

    def raw_scores(half):
        q = q_ref[0, half * tq:(half + 1) * tq, :]
        lo = lax.broadcasted_iota(jnp.int32, q.shape, 1) < DA_HEAD_DIM
        zero = jnp.zeros_like(q)
        return _nt_dot(k_ref[0], jnp.concatenate([jnp.where(lo, q, zero), jnp.where(lo, zero, q)], axis=0))

    def biased(raw, half):
        bias = bias_ref[2 * pa + half]
        s = jnp.concatenate([raw[:, :tq] - bias, raw[:, tq:] - bias], axis=1)
        return s, jnp.max(s, axis=0, keepdims=True)

    def epilogue(a):
        a1, a2 = a[:, :tq], a[:, tq:]
        o_t = a1[:w] * (1.0 / a1[w:w + 1]) - a2[:w] * (lam_ref[0] / a2[w:w + 1])
        return (_rms(o_t.T, g_ref[...]) * out_scale).astype(o_ref.dtype)

    raw0 = raw_scores(0)
    p_prev = jnp.exp2(s_ref[...] - m_ref[...]).astype(BF16)
    o_ref[0, :tq, :] = epilogue(a_ref[...])
    a_prev = _dot(vext_ref[bb % 2], p_prev)
    s0, m0 = biased(raw0, 0)
    raw1 = raw_scores(1)
    p0 = jnp.exp2(s0 - m0).astype(BF16)
    o_ref[0, tq:, :] = epilogue(a_prev)
    a_ref[...] = _dot(vext_ref[ba % 2], p0)
    s1, m1 = biased(raw1, 1)
    s_ref[...] = s1
    m_ref[...] = m1


def _diff_attn(proj3, slopes, lam, sub_g, col_q, col_k, col_v, lambda_init):
    b, s, _ = proj3.shape
    w = 2 * DA_HEAD_DIM
    tq = _tile(s, 512) // 2
    npb = s // (2 * tq)
    n_pairs = b * npb
    smem = pl.BlockSpec(memory_space=pltpu.SMEM)

    def pair_in(g):
        ga = jnp.minimum(g, n_pairs - 1)
        return ga // npb, ga % npb

    def pair_out(g):
        gb = jnp.maximum(g - 1, 0)
        return gb // npb, gb % npb

    return pl.pallas_call(
        functools.partial(_diff_attn_kernel, out_scale=1.0 - lambda_init),
        out_shape=jax.ShapeDtypeStruct((b, s, DA_HEADS * w), BF16),
        grid=(DA_HEADS, n_pairs + 1),
        in_specs=[
            smem, smem,
            pl.BlockSpec((1, 2 * tq, w), lambda h, g: (*pair_in(g), col_q // w + h)),
            pl.BlockSpec((1, s, w), lambda h, g: (pair_in(g)[0], 0, col_k // w + h)),
            pl.BlockSpec((1, s, w), lambda h, g: (pair_in(g)[0], 0, col_v // w + h)),
            pl.BlockSpec((1, w), lambda h, g: (0, 0)),
        ],
        out_specs=pl.BlockSpec((1, 2 * tq, w), lambda h, g: (*pair_out(g), h)),
        scratch_shapes=[
            pltpu.VMEM((s // tq, s, tq), F32),
            pltpu.VMEM((2, w + BF16_SUBLANES, s), BF16),
            pltpu.VMEM((s, 2 * tq), F32),
            pltpu.VMEM((1, 2 * tq), F32),
            pltpu.VMEM((w + BF16_SUBLANES, 2 * tq), F32),
        ],
        compiler_params=_cparams("parallel", "arbitrary"),
        name="diff_attn",
    )(slopes, lam, proj3, proj3, proj3, sub_g)


def _mem_kv_kernel(m_ref, g_ref, w_ref, o_ref):
    mn = _rms(m_ref[...], g_ref[0]).astype(BF16)
    o_ref[0] = _dot(mn, w_ref[0]).astype(o_ref.dtype)


def _mem_kv(mem2, mem_g, w_kv):
    rows, d = mem2.shape
    depth, _, n = w_kv.shape
    tn = _tile(n, 1024)
    return pl.pallas_call(
        _mem_kv_kernel,
        out_shape=jax.ShapeDtypeStruct((depth, rows, n), BF16),
        grid=(depth, n // tn),
        in_specs=[
            pl.BlockSpec((rows, d), lambda l, j: (0, 0)),
            pl.BlockSpec((1, 1, d), lambda l, j: (l, 0, 0)),
            pl.BlockSpec((1, d, tn), lambda l, j: (l, 0, j)),
        ],
        out_specs=pl.BlockSpec((1, rows, tn), lambda l, j: (l, 0, j)),
        compiler_params=_cparams("parallel", "parallel"),
        name="mem_kv",
    )(mem2, mem_g, w_kv)


def _xattn_kernel(q_ref, kv_ref, o_ref):
    xw = q_ref.shape[2]
    hd = xw // XA_HEADS
    for h in range(XA_HEADS):
        sl = slice(h * hd, (h + 1) * hd)
        qh = q_ref[0, :, sl] * (hd ** -0.5)
        s = _nt_dot(qh, kv_ref[0, :, sl])
        p = jnp.exp(s - jnp.max(s, axis=-1, keepdims=True))
        p = p * (1.0 / jnp.sum(p, axis=-1, keepdims=True))
        o_ref[0, :, sl] = _dot(p.astype(BF16), kv_ref[0, :, xw + h * hd:xw + (h + 1) * hd]).astype(o_ref.dtype)


def _xattn(proj3, kv, col_q, xw, layer):
    b, s, _ = proj3.shape
    m = kv.shape[1] // b
    tq = _tile(s, 512)
    return pl.pallas_call(
        _xattn_kernel,
        out_shape=jax.ShapeDtypeStruct((b, s, xw), BF16),
        grid=(b, s // tq),
        in_specs=[
            pl.BlockSpec((1, tq, xw), lambda i, t: (i, t, col_q // xw)),
            pl.BlockSpec((1, m, 2 * xw), lambda i, t: (layer, i, 0)),
        ],
        out_specs=pl.BlockSpec((1, tq, xw), lambda i, t: (i, t, 0)),
        compiler_params=_cparams("parallel", "parallel"),
        name="mem_xattn",
    )(proj3, kv)


def _merge_kernel(yda_ref, yssm_ref, yxa_ref, gt_ref, x_ref, wb_ref, wo_ref, o_ref):
    d = x_ref.shape[1]
    n_da, n_ssm = yda_ref.shape[1], yssm_ref.shape[1]
    gate = jax.nn.sigmoid(gt_ref[...].astype(F32))
    merged = (gate[:, :d] * _dot(yda_ref[...], wb_ref[:n_da, :])
              + gate[:, d:2 * d] * _dot(yssm_ref[...], wb_ref[n_da:n_da + n_ssm, :])
              + gate[:, 2 * d:] * _dot(yxa_ref[...], wb_ref[n_da + n_ssm:, :]))
    o_ref[...] = x_ref[...] + _dot(merged.astype(BF16), wo_ref[...])


def _merge(y_da, y_ssm, y_xa, proj2, col_gates, x2, w_branch, w_out, layer):
    t, d = x2.shape
    tm = _tile(t, 512)
    assert col_gates % (3 * d) == 0
    row = lambda i: (i, 0)
    whole = lambda i: (layer, 0, 0)
    return pl.pallas_call(
        _merge_kernel,
        out_shape=jax.ShapeDtypeStruct((t, d), F32),
        grid=(t // tm,),
        in_specs=[
            pl.BlockSpec((tm, y_da.shape[1]), row),
            pl.BlockSpec((tm, y_ssm.shape[1]), row),
            pl.BlockSpec((tm, y_xa.shape[1]), row),
            pl.BlockSpec((tm, 3 * d), lambda i: (i, col_gates // (3 * d))),
            pl.BlockSpec((tm, d), row),
            pl.BlockSpec((None,) + w_branch.shape[1:], whole),
            pl.BlockSpec((None,) + w_out.shape[1:], whole),
        ],
        out_specs=pl.BlockSpec((tm, d), row),
        compiler_params=_cparams("parallel"),
        name="merge_out",
    )(y_da, y_ssm, y_xa, proj2, x2, w_branch, w_out)


def _ffn_kernel(x_ref, g_ref, wi_ref, wo_ref, fg_ref, o_ref, *, final_norm, chunks):
    hid = wo_ref.shape[0]
    x = x_ref[...]
    h = _rms(x, g_ref[...]).astype(BF16)
    y = x
    for c0, c1 in chunks:
        gate = _dot(h, wi_ref[:, c0:c1])
        up = _dot(h, wi_ref[:, hid + c0:hid + c1])
        act = (gate * jax.nn.sigmoid(gate) * up).astype(BF16)
        y = y + _dot(act, wo_ref[c0:c1, :])
    if final_norm:
        y = _rms(y, fg_ref[...])
    o_ref[...] = y


def _ffn(x2, g, w_in, w_out, final_g, final_norm, layer):
    t, d = x2.shape
    hid = w_out.shape[1]
    tm = _tile(t, 512)
    assert hid % MXU_DIM == 0
    step = 3 * MXU_DIM
    chunks = tuple((c, min(c + step, hid)) for c in range(0, hid, step))
    resident = pl.Buffered(1)
    return pl.pallas_call(
        functools.partial(_ffn_kernel, final_norm=final_norm, chunks=chunks),
        out_shape=jax.ShapeDtypeStruct((t, d), F32),
        grid=(t // tm,),
        in_specs=[
            pl.BlockSpec((tm, d), lambda i: (i, 0)),
            pl.BlockSpec((1, d), lambda i: (0, 0)),
            pl.BlockSpec((None,) + w_in.shape[1:], lambda i: (layer, 0, 0), pipeline_mode=resident),
            pl.BlockSpec((None,) + w_out.shape[1:], lambda i: (layer, 0, 0), pipeline_mode=resident),
            pl.BlockSpec((1, d), lambda i: (0, 0)),
        ],
        out_specs=pl.BlockSpec((tm, d), lambda i: (i, 0)),
        compiler_params=_cparams("parallel"),
        name="ffn",
    )(x2, g, w_in, w_out, final_g)


def _alibi_slopes(n_heads):
    start = 2.0 ** (-8.0 / n_heads)
    return np.array([start ** (i + 1) for i in range(n_heads)], dtype=np.float32)


def kernel(x, mem, mix_norm_g, w_in, da_lambda_q1, da_lambda_k1, da_lambda_q2, da_lambda_k2, da_subln_g, ssm_conv_w, ssm_conv_b, ssm_dt_bias, ssm_A_log, ssm_D, ssm_norm_g, mem_norm_g, w_mem_kv, w_branch, w_out, ffn_norm_g, w_ffn_in, w_ffn_out, final_norm_g):
    b, s, d = x.shape
    depth = w_in.shape[0]
    t = b * s
    m = mem.shape[1]
    da_cols = DA_HEADS * 2 * DA_HEAD_DIM
    inner = SSM_HEADS * SSM_HEAD_DIM
    conv_ch = inner + 2 * SSM_GROUPS * SSM_STATE
    n_dt = 2 * SSM_HEADS
    xw = d

    o_q, o_k, o_v = 0, da_cols, 2 * da_cols
    o_z = 3 * da_cols
    o_xbc = o_z + inner
    o_dt = o_xbc + conv_ch
    o_xq = o_dt + n_dt
    o_gt = o_xq + xw
    assert n_dt <= LANES
    col_scale = jnp.ones((w_in.shape[2],), F32).at[o_q:o_k].set(DA_HEAD_DIM ** -0.5 * LOG2E)
    w_in_b = (w_in * col_scale).astype(BF16)
    w_plain = jnp.concatenate([w_in_b[:, :, :o_xbc], w_in_b[:, :, o_xq:]], axis=2)
    w_conv = w_in_b[:, :, o_xbc:o_dt]
    c_q, c_k, c_v, c_z = o_q, o_k, o_v, o_z
    c_xq = o_xbc
    c_gt = c_xq + xw

    w_kv_b = w_mem_kv.astype(BF16)
    w_branch_b = w_branch.astype(BF16)
    w_out_b = w_out.astype(BF16)
    w_ffn_in_b = w_ffn_in.astype(BF16)
    w_ffn_out_b = w_ffn_out.astype(BF16)

    pad_lanes = lambda v: jnp.pad(v.reshape(depth, 1, n_dt), ((0, 0), (0, 0), (0, LANES - n_dt)))
    dt_bias_p = pad_lanes(ssm_dt_bias.astype(F32))
    a_log_p = pad_lanes(ssm_A_log.astype(F32))
    d_skip = jnp.repeat(ssm_D.astype(F32), SSM_HEAD_DIM, axis=1).reshape(depth, 1, inner)
    slopes = jnp.asarray(_alibi_slopes(DA_HEADS))
    conv_b3 = ssm_conv_b.reshape(depth, 1, conv_ch)

    kv_all = _mem_kv(mem.reshape(b * m, d), mem_norm_g.reshape(depth, 1, d), w_kv_b)

    x2 = x.reshape(t, d)
    for i in range(depth):
        lambda_init = 0.8 - 0.6 * math.exp(-0.3 * i)
        lam = (jnp.exp(jnp.sum(da_lambda_q1[i] * da_lambda_k1[i]).astype(F32))
               - jnp.exp(jnp.sum(da_lambda_q2[i] * da_lambda_k2[i]).astype(F32))
               + lambda_init).reshape(1)

        proj2, xbc2, dt_raw = _in_proj(x2, mix_norm_g[i].reshape(1, d), w_plain, w_conv, w_in_b, o_dt,
                                       ssm_conv_w, conv_b3, i, seq_len=s, steps=IN_PROJ_STEPS)
        proj3 = proj2.reshape(b, s, -1)
        xbc3 = xbc2.reshape(b, s, conv_ch)
        dt3 = dt_raw.reshape(b, s, LANES)

        y_bwd = _ssd(xbc3, 0, dt3, dt_bias_p[i], a_log_p[i], backward=True)
        y_ssm = _ssd(xbc3, 0, dt3, dt_bias_p[i], a_log_p[i], backward=False,
                     fused=(y_bwd, proj3, c_z, d_skip[i], ssm_norm_g[i].reshape(1, inner)))

        y_da = _diff_attn(proj3, slopes, lam, da_subln_g[i].reshape(1, 2 * DA_HEAD_DIM),
                          c_q, c_k, c_v, lambda_init)
        y_xa = _xattn(proj3, kv_all, c_xq, xw, i)

        x2 = _merge(y_da.reshape(t, -1), y_ssm.reshape(t, inner), y_xa.reshape(t, xw),
                    proj2, c_gt, x2, w_branch_b, w_out_b, i)
        x2 = _ffn(x2, ffn_norm_g[i].reshape(1, d), w_ffn_in_b, w_ffn_out_b,
                  final_norm_g.reshape(1, d), final_norm=(i == depth - 1), layer=i)
    return x2.reshape(b, s, d)
```

```python
import functools
import math

import numpy as np
import jax
import jax.numpy as jnp
from jax import lax
from jax.experimental import pallas as pl
from jax.experimental.pallas import tpu as pltpu

F32 = jnp.float32
BF16 = jnp.bfloat16

EPS = 1e-6
LOG2E = math.log2(math.e)
LANES = 128
MXU_DIM = 256
F32_SUBLANES = 8
BF16_SUBLANES = 16
DA_HEADS = 8
DA_HEAD_DIM = 64
DA_KEY_CHUNK = 256
SSM_HEADS = 32
SSM_HEAD_DIM = 64
SSM_GROUPS = 4
SSM_STATE = 128
SSM_CHUNK = 128
SSM_CONV = 5
XA_HEADS = 4
VMEM_LIMIT = 56 * 1024 * 1024


def _cparams(*sem):
    return pltpu.CompilerParams(dimension_semantics=sem, vmem_limit_bytes=VMEM_LIMIT)


def _rms(x, g):
    ms = jnp.mean(x * x, axis=-1, keepdims=True)
    return x * lax.rsqrt(ms + EPS) * g


def _nt_dot(a, b):
    return lax.dot_general(a, b, (((1,), (1,)), ((), ())), preferred_element_type=F32)


def _dot(a, b):
    return jnp.dot(a, b, preferred_element_type=F32)


def _tile(n, want):
    t = min(n, want)
    assert n % t == 0, (n, t)
    return t


def _w_in_prep_kernel(a_ref, b_ref, wh_ref, wt_ref, wdt_ref, *, n_head, q_blocks, q_scale, shift):
    j = pl.program_id(1)

    @pl.when(j < n_head)
    def _():
        wh_ref[...] = (a_ref[...] * jnp.where(j < q_blocks, q_scale, 1.0)).astype(wh_ref.dtype)

    @pl.when(j == n_head - 1)
    def _():
        wdt_ref[...] = b_ref[...].astype(wdt_ref.dtype)

    @pl.when(j >= n_head)
    def _():
        wt_ref[...] = jnp.concatenate([a_ref[:, shift:], b_ref[:, :shift]], axis=1).astype(wt_ref.dtype)


def _w_in_prep(w_in, n_head_cols, n_skip, q_cols, q_scale):
    depth, d, n = w_in.shape
    w = 1024
    n_tail_cols = n - n_head_cols - n_skip
    assert n_head_cols % w == 0 and n_tail_cols % w == 0 and q_cols % w == 0 and 0 < n_skip < LANES
    n_head, n_tail = n_head_cols // w, n_tail_cols // w
    return pl.pallas_call(
        functools.partial(_w_in_prep_kernel, n_head=n_head, q_blocks=q_cols // w, q_scale=q_scale, shift=n_skip),
        out_shape=(jax.ShapeDtypeStruct((depth, d, n_head_cols), BF16),
                   jax.ShapeDtypeStruct((depth, d, n_tail_cols), BF16),
                   jax.ShapeDtypeStruct((depth, d, LANES), BF16)),
        grid=(depth, n_head + n_tail),
        in_specs=[
            pl.BlockSpec((None, d, w), lambda l, j: (l, 0, j)),
            pl.BlockSpec((None, d, LANES), lambda l, j: (l, 0, (j + 1) * (w // LANES))),
        ],
        out_specs=(
            pl.BlockSpec((None, d, w), lambda l, j: (l, 0, jnp.minimum(j, n_head - 1))),
            pl.BlockSpec((None, d, w), lambda l, j: (l, 0, jnp.maximum(j - n_head, 0))),
            pl.BlockSpec((None, d, LANES), lambda l, j: (l, 0, 0)),
        ),
        compiler_params=_cparams("parallel", "arbitrary"),
        name="w_in_prep",
    )(w_in, w_in)


def _conv_silu(y, w_ref, b_ref, cols):
    s_len = y.shape[0]
    pad = SSM_CONV // 2
    assert pad <= F32_SUBLANES
    edge = jnp.zeros((F32_SUBLANES, y.shape[1]), F32)
    yp = jnp.concatenate([edge, y, edge], axis=0)
    n = yp.shape[0]
    w = lambda k: w_ref[k:k + 1, cols]
    later = earlier = None
    for d in range(pad, 0, -1):
        nxt, prv = yp * w(pad + d), yp * w(pad - d)
        later = pltpu.roll(nxt if later is None else later + nxt, n - 1, axis=0)
        earlier = pltpu.roll(prv if earlier is None else earlier + prv, 1, axis=0)
    acc = y * w(pad) + b_ref[:, cols] + (later + earlier)[F32_SUBLANES:F32_SUBLANES + s_len]
    return acc * jax.nn.sigmoid(acc)


def _in_proj_kernel(x_ref, g_ref, wh_ref, wt_ref, wc_ref, wdt_ref, cw_ref, cb_ref, o_ref, oc_ref, dt_ref, h_ref,
                    *, n_head, n_conv):
    j = pl.program_id(1)

    @pl.when(j == 0)
    def _():
        h = _rms(x_ref[...], g_ref[...]).astype(BF16)
        h_ref[...] = h
        dt_ref[...] = _dot(h, wdt_ref[...])

    def step(w_ref, conv):
        h = h_ref[...]
        if conv:
            y = _dot(h, wc_ref[...])
            oc_ref[...] = _conv_silu(y, cw_ref, cb_ref, slice(None)).astype(oc_ref.dtype)
        o_ref[...] = _dot(h, w_ref[...]).astype(o_ref.dtype)

    lo, hi = min(n_head, n_conv), max(n_head, n_conv)
    pl.when(j < lo)(lambda: step(wh_ref, True))
    pl.when(jnp.logical_and(j >= lo, j < hi))(lambda: step(wh_ref if n_head > n_conv else wt_ref, n_conv > n_head))
    pl.when(j >= hi)(lambda: step(wt_ref, False))


def _in_proj(x2, g, w_head, w_tail, w_dt, conv_w, conv_b, layer, *, seq_len, conv_col0):
    t, d = x2.shape
    nh, nt = w_head.shape[2], w_tail.shape[2]
    n_cv = nh - conv_col0
    tm, tp, tc = seq_len, 1024, 512
    assert t % tm == 0 and conv_col0 % tp == 0 and nt % tp == 0 and n_cv % tc == 0 and conv_col0 % tc == 0
    n_head, n_tail, n_conv = conv_col0 // tp, nt // tp, n_cv // tc
    steps = n_head + n_tail
    assert n_conv <= steps
    conv_blk = lambda j: jnp.minimum(j, n_conv - 1)
    return pl.pallas_call(
        functools.partial(_in_proj_kernel, n_head=n_head, n_conv=n_conv),
        out_shape=(jax.ShapeDtypeStruct((t, conv_col0 + nt), BF16), jax.ShapeDtypeStruct((t, n_cv), BF16),
                   jax.ShapeDtypeStruct((t, LANES), F32)),
        grid=(t // tm, steps),
        in_specs=[
            pl.BlockSpec((tm, d), lambda i, j: (i, 0), pipeline_mode=pl.Buffered(1)),
            pl.BlockSpec((1, d), lambda i, j: (0, 0)),
            pl.BlockSpec((None, d, tp), lambda i, j: (layer, 0, jnp.minimum(j, n_head - 1))),
            pl.BlockSpec((None, d, tp), lambda i, j: (layer, 0, jnp.maximum(j - n_head, 0))),
            pl.BlockSpec((None, d, tc), lambda i, j: (layer, 0, conv_col0 // tc + conv_blk(j))),
            pl.BlockSpec((None, d, LANES), lambda i, j: (layer, 0, 0)),
            pl.BlockSpec((None, SSM_CONV, tc), lambda i, j: (layer, 0, conv_blk(j))),
            pl.BlockSpec((None, 1, tc), lambda i, j: (layer, 0, conv_blk(j))),
        ],
        out_specs=(
            pl.BlockSpec((tm, tp), lambda i, j: (i, j)),
            pl.BlockSpec((tm, tc), lambda i, j: (i, conv_blk(j))),
            pl.BlockSpec((tm, LANES), lambda i, j: (i, 0)),
        ),
        scratch_shapes=[pltpu.VMEM((tm, d), BF16)],
        compiler_params=_cparams("parallel", "arbitrary"),
        name="in_proj",
    )(x2, g, w_head, w_tail, w_head, w_dt, conv_w, conv_b)


def _softplus(x):
    return jnp.maximum(x, 0.0) + jnp.log(1.0 + jnp.exp(-jnp.abs(x)))


def _ssd_kernel(*refs, backward, fuse):
    if fuse:
        xs0, xs1, b_ref, c_ref, dt_ref, dtb_ref, alog_ref, yb_ref, z0, z1, dsk_ref, ng_ref, o_ref, st_ref = refs
    else:
        xs0, xs1, b_ref, c_ref, dt_ref, dtb_ref, alog_ref, o_ref, st_ref = refs
    q = SSM_CHUNK

    def cols(halves, sl):
        hw = halves[0].shape[2]
        k = sl.start // hw
        return halves[k][0, :, sl.start - k * hw:sl.stop - k * hw]

    @pl.when(pl.program_id(1) == 0)
    def _():
        st_ref[...] = jnp.zeros_like(st_ref)

    dt = _softplus(dt_ref[0] + dtb_ref[...])
    a = dt * (-LOG2E * jnp.exp(alog_ref[...]))
    row = lax.broadcasted_iota(jnp.int32, (q, LANES), 0)
    cs = a
    k = 1
    while k < q:
        if backward:
            cs = cs + jnp.where(row < q - k, pltpu.roll(cs, q - k, axis=0), 0.0)
        else:
            cs = cs + jnp.where(row >= k, pltpu.roll(cs, k, axis=0), 0.0)
        k *= 2
    cs_t = cs.T
    dt_t = dt.T
    end = 0 if backward else q - 1
    w_state_t = dt_t * jnp.exp2(cs_t[:, end:end + 1] - cs_t)

    li = lax.broadcasted_iota(jnp.int32, (q, q), 0)
    si = lax.broadcasted_iota(jnp.int32, (q, q), 1)
    tri = (li <= si) if backward else (li >= si)
    lo = lax.broadcasted_iota(jnp.int32, (q, LANES), 1) < SSM_HEAD_DIM
    lane0 = SSM_HEADS if backward else 0
    pairs_per_group = SSM_HEADS // SSM_GROUPS // 2
    gw = SSM_HEADS // SSM_GROUPS * SSM_HEAD_DIM

    def blockdiag(v):
        zero = jnp.zeros_like(v)
        return jnp.concatenate([jnp.where(lo, v, zero), jnp.where(lo, zero, v)], axis=0)

    src_t = cs_t - jnp.log2(dt_t)

    for g in range(SSM_GROUPS):
        gs = slice(g * gw, (g + 1) * gw)
        bg = b_ref[0, :, g * SSM_STATE:(g + 1) * SSM_STATE]
        cg = c_ref[0, :, g * SSM_STATE:(g + 1) * SSM_STATE]
        cb = _nt_dot(cg, bg)
        bg_t = bg.astype(F32).T
        y_in = _dot(cg, st_ref[:, gs].astype(BF16))
        ys = []
        for pp in range(pairs_per_group):
            p = g * pairs_per_group + pp
            sl = slice(p * LANES, (p + 1) * LANES)
            xbd = blockdiag(cols((xs0, xs1), sl))
            m_l, b_l, e_in = [], [], []
            for h in (2 * p, 2 * p + 1):
                r = lane0 + h
                col = jnp.broadcast_to(cs[:, r:r + 1], (q, q))
                m_l.append((cb * jnp.exp2(jnp.where(tri, col - src_t[r:r + 1, :], -jnp.inf))).astype(BF16))
                b_l.append((bg_t * w_state_t[r:r + 1, :]).astype(BF16))
                e_in.append(jnp.exp2(col))
            e_pair = jnp.where(lo, e_in[0], e_in[1])
            y = _dot(jnp.concatenate(m_l, axis=1), xbd) + y_in[:, pp * LANES:(pp + 1) * LANES] * e_pair
            st_new = _dot(jnp.concatenate(b_l, axis=1), xbd)
            st_ref[:, sl] = st_ref[:, sl] * e_pair[end:end + 1, :] + st_new
            if fuse:
                ys.append(y)
            else:
                o_ref[0, :, sl] = y
        if fuse:
            yg = (jnp.concatenate(ys, axis=1) + yb_ref[0, :, gs]
                  + cols((xs0, xs1), gs).astype(F32) * dsk_ref[:, gs])
            zg = cols((z0, z1), gs).astype(F32)
            yg = yg * (zg * jax.nn.sigmoid(zg))
            o_ref[0, :, gs] = _rms(yg, ng_ref[:, gs]).astype(o_ref.dtype)


def _ssd(xbc, col_xbc, dt_raw, dt_bias, a_log, *, backward, fused=None):
    b, s, _ = xbc.shape
    q = SSM_CHUNK
    nc = s // q
    inner = SSM_HEADS * SSM_HEAD_DIM
    hw = inner // 2
    bc = SSM_GROUPS * SSM_STATE
    assert col_xbc % hw == 0 and (col_xbc + inner) % bc == 0
    cix = (lambda c: nc - 1 - c) if backward else (lambda c: c)

    def halves(col):
        return [pl.BlockSpec((1, q, hw), lambda i, c, k=k: (i, cix(c), col // hw + k)) for k in range(2)]

    in_specs = halves(col_xbc) + [
        pl.BlockSpec((1, q, bc), lambda i, c: (i, cix(c), (col_xbc + inner) // bc)),
        pl.BlockSpec((1, q, bc), lambda i, c: (i, cix(c), (col_xbc + inner) // bc + 1)),
        pl.BlockSpec((1, q, LANES), lambda i, c: (i, cix(c), 0)),
        pl.BlockSpec((1, LANES), lambda i, c: (0, 0)),
        pl.BlockSpec((1, LANES), lambda i, c: (0, 0)),
    ]
    args = [xbc, xbc, xbc, xbc, dt_raw, dt_bias, a_log]
    if fused is not None:
        y_bwd, z_src, col_z, d_skip, norm_g = fused
        assert col_z % hw == 0
        in_specs += [pl.BlockSpec((1, q, inner), lambda i, c: (i, cix(c), 0))] + halves(col_z) + [
            pl.BlockSpec((1, inner), lambda i, c: (0, 0)),
            pl.BlockSpec((1, inner), lambda i, c: (0, 0)),
        ]
        args += [y_bwd, z_src, z_src, d_skip, norm_g]
    return pl.pallas_call(
        functools.partial(_ssd_kernel, backward=backward, fuse=fused is not None),
        out_shape=jax.ShapeDtypeStruct((b, s, inner), BF16 if fused is not None else F32),
        grid=(b, nc),
        in_specs=in_specs,
        out_specs=pl.BlockSpec((1, q, inner), lambda i, c: (i, cix(c), 0)),
        scratch_shapes=[pltpu.VMEM((SSM_STATE, inner), F32)],
        compiler_params=_cparams("parallel", "arbitrary"),
        name="ssd_bwd" if backward else "ssd_fwd",
    )(*args)


def _diff_attn_kernel(slopes_ref, lam_ref, q_ref, k_ref, v_ref, g_ref, o_ref, bias_ref, vext_ref, s_ref, m_ref, a_ref,
                      *, out_scale):
    tq = q_ref.shape[1] // 2
    s_len = k_ref.shape[1]
    w = v_ref.shape[2]
    npb = s_len // (2 * tq)
    h, g = pl.program_id(0), pl.program_id(1)
    n_pairs = pl.num_programs(1) - 1
    ga = jnp.minimum(g, n_pairs - 1)
    ba, pa = ga // npb, ga % npb
    bb = jnp.maximum(g - 1, 0) // npb

    @pl.when(g == 0)
    def _():
        s_ref[...] = jnp.zeros_like(s_ref)
        m_ref[...] = jnp.zeros_like(m_ref)
        a_ref[...] = jnp.ones_like(a_ref)

    @pl.when(jnp.logical_and(ba == 0, g < n_pairs))
    def _():
        kpos = lax.broadcasted_iota(jnp.int32, (s_len, tq), 0)
        for half in range(2):
            qpos = (2 * pa + half) * tq + lax.broadcasted_iota(jnp.int32, (s_len, tq), 1)
            bias_ref[2 * pa + half] = (slopes_ref[h] * LOG2E) * jnp.abs(qpos - kpos).astype(F32)

    @pl.when(jnp.logical_and(pa == 0, g < n_pairs))
    def _():
        vext_ref[ba % 2, :w, :] = v_ref[0].astype(F32).T.astype(BF16)
        vext_ref[ba % 2, w:, :] = jnp.ones((vext_ref.shape[1] - w, s_len), BF16)

    def raw_scores(half):
        q = q_ref[0, half * tq:(half + 1) * tq, :]
        lo = lax.broadcasted_iota(jnp.int32, q.shape, 1) < DA_HEAD_DIM
        zero = jnp.zeros_like(q)
        return _nt_dot(k_ref[0], jnp.concatenate([jnp.where(lo, q, zero), jnp.where(lo, zero, q)], axis=0))

    def biased(raw, half):
        bias = bias_ref[2 * pa + half]
        s = jnp.concatenate([raw[:, :tq] - bias, raw[:, tq:] - bias], axis=1)
        return s, jnp.max(s, axis=0, keepdims=True)

    def epilogue(a):
        a1, a2 = a[:, :tq], a[:, tq:]
        o_t = a1[:w] * (1.0 / a1[w:w + 1]) - a2[:w] * (lam_ref[0] / a2[w:w + 1])
        return (_rms(o_t.T, g_ref[...]) * out_scale).astype(o_ref.dtype)

    raw0 = raw_scores(0)
    p_prev = jnp.exp2(s_ref[...] - m_ref[...]).astype(BF16)
    o_ref[0, :tq, :] = epilogue(a_ref[...])
    a_prev = _dot(vext_ref[bb % 2], p_prev)
    s0, m0 = biased(raw0, 0)
    raw1 = raw_scores(1)
    p0 = jnp.exp2(s0 - m0).astype(BF16)
    o_ref[0, tq:, :] = epilogue(a_prev)
    a_ref[...] = _dot(vext_ref[ba % 2], p0)
    s1, m1 = biased(raw1, 1)
    s_ref[...] = s1
    m_ref[...] = m1


def _diff_attn(proj3, slopes, lam, sub_g, col_q, col_k, col_v, lambda_init):
    b, s, _ = proj3.shape
    w = 2 * DA_HEAD_DIM
    tq = _tile(s, 512) // 2
    npb = s // (2 * tq)
    n_pairs = b * npb
    smem = pl.BlockSpec(memory_space=pltpu.SMEM)

    def pair_in(g):
        ga = jnp.minimum(g, n_pairs - 1)
        return ga // npb, ga % npb

    def pair_out(g):
        gb = jnp.maximum(g - 1, 0)
        return gb // npb, gb % npb

    return pl.pallas_call(
        functools.partial(_diff_attn_kernel, out_scale=1.0 - lambda_init),
        out_shape=jax.ShapeDtypeStruct((b, s, DA_HEADS * w), BF16),
        grid=(DA_HEADS, n_pairs + 1),
        in_specs=[
            smem, smem,
            pl.BlockSpec((1, 2 * tq, w), lambda h, g: (*pair_in(g), col_q // w + h)),
            pl.BlockSpec((1, s, w), lambda h, g: (pair_in(g)[0], 0, col_k // w + h)),
            pl.BlockSpec((1, s, w), lambda h, g: (pair_in(g)[0], 0, col_v // w + h)),
            pl.BlockSpec((1, w), lambda h, g: (0, 0)),
        ],
        out_specs=pl.BlockSpec((1, 2 * tq, w), lambda h, g: (*pair_out(g), h)),
        scratch_shapes=[
            pltpu.VMEM((s // tq, s, tq), F32),
            pltpu.VMEM((2, w + BF16_SUBLANES, s), BF16),
            pltpu.VMEM((s, 2 * tq), F32),
            pltpu.VMEM((1, 2 * tq), F32),
            pltpu.VMEM((w + BF16_SUBLANES, 2 * tq), F32),
        ],
        compiler_params=_cparams("parallel", "arbitrary"),
        name="diff_attn",
    )(slopes, lam, proj3, proj3, proj3, sub_g)


def _mem_kv_kernel(m_ref, g_ref, w_ref, o_ref):
    mn = _rms(m_ref[...], g_ref[0]).astype(BF16)
    o_ref[0] = _dot(mn, w_ref[0]).astype(o_ref.dtype)


def _mem_kv(mem2, mem_g, w_kv):
    rows, d = mem2.shape
    depth, _, n = w_kv.shape
    tn = _tile(n, 1024)
    return pl.pallas_call(
        _mem_kv_kernel,
        out_shape=jax.ShapeDtypeStruct((depth, rows, n), BF16),
        grid=(depth, n // tn),
        in_specs=[
            pl.BlockSpec((rows, d), lambda l, j: (0, 0)),
            pl.BlockSpec((1, 1, d), lambda l, j: (l, 0, 0)),
            pl.BlockSpec((1, d, tn), lambda l, j: (l, 0, j)),
        ],
        out_specs=pl.BlockSpec((1, rows, tn), lambda l, j: (l, 0, j)),
        compiler_params=_cparams("parallel", "parallel"),
        name="mem_kv",
    )(mem2, mem_g, w_kv)


def _xattn_kernel(q_ref, kv_ref, o_ref):
    xw = q_ref.shape[2]
    hd = xw // XA_HEADS
    for h in range(XA_HEADS):
        sl = slice(h * hd, (h + 1) * hd)
        qh = q_ref[0, :, sl] * (hd ** -0.5)
        s = _nt_dot(qh, kv_ref[0, :, sl])
        p = jnp.exp(s - jnp.max(s, axis=-1, keepdims=True))
        p = p * (1.0 / jnp.sum(p, axis=-1, keepdims=True))
        o_ref[0, :, sl] = _dot(p.astype(BF16), kv_ref[0, :, xw + h * hd:xw + (h + 1) * hd]).astype(o_ref.dtype)


def _xattn(proj3, kv, col_q, xw, layer):
    b, s, _ = proj3.shape
    m = kv.shape[1] // b
    tq = _tile(s, 512)
    return pl.pallas_call(
        _xattn_kernel,
        out_shape=jax.ShapeDtypeStruct((b, s, xw), BF16),
        grid=(b, s // tq),
        in_specs=[
            pl.BlockSpec((1, tq, xw), lambda i, t: (i, t, col_q // xw)),
            pl.BlockSpec((1, m, 2 * xw), lambda i, t: (layer, i, 0)),
        ],
        out_specs=pl.BlockSpec((1, tq, xw), lambda i, t: (i, t, 0)),
        compiler_params=_cparams("parallel", "parallel"),
        name="mem_xattn",
    )(proj3, kv)


def _merge_kernel(yda_ref, yssm_ref, yxa_ref, gt_ref, x_ref, wb_ref, wo_ref, o_ref):
    d = x_ref.shape[1]
    n_da, n_ssm = yda_ref.shape[1], yssm_ref.shape[1]
    gate = jax.nn.sigmoid(gt_ref[...].astype(F32))
    merged = (gate[:, :d] * _dot(yda_ref[...], wb_ref[:n_da, :])
              + gate[:, d:2 * d] * _dot(yssm_ref[...], wb_ref[n_da:n_da + n_ssm, :])
              + gate[:, 2 * d:] * _dot(yxa_ref[...], wb_ref[n_da + n_ssm:, :]))
    o_ref[...] = x_ref[...] + _dot(merged.astype(BF16), wo_ref[...])


def _merge(y_da, y_ssm, y_xa, proj2, col_gates, x2, w_branch, w_out, layer):
    t, d = x2.shape
    tm = _tile(t, 512)
    assert col_gates % (3 * d) == 0
    row = lambda i: (i, 0)
    whole = lambda i: (layer, 0, 0)
    return pl.pallas_call(
        _merge_kernel,
        out_shape=jax.ShapeDtypeStruct((t, d), F32),
        grid=(t // tm,),
        in_specs=[
            pl.BlockSpec((tm, y_da.shape[1]), row),
            pl.BlockSpec((tm, y_ssm.shape[1]), row),
            pl.BlockSpec((tm, y_xa.shape[1]), row),
            pl.BlockSpec((tm, 3 * d), lambda i: (i, col_gates // (3 * d))),
            pl.BlockSpec((tm, d), row),
            pl.BlockSpec((None,) + w_branch.shape[1:], whole),
            pl.BlockSpec((None,) + w_out.shape[1:], whole),
        ],
        out_specs=pl.BlockSpec((tm, d), row),
        compiler_params=_cparams("parallel"),
        name="merge_out",
    )(y_da, y_ssm, y_xa, proj2, x2, w_branch, w_out)


def _ffn_kernel(x_ref, g_ref, wi_ref, wo_ref, fg_ref, o_ref, *, final_norm, chunks):
    hid = wo_ref.shape[0]
    x = x_ref[...]
    h = _rms(x, g_ref[...]).astype(BF16)
    y = x
    for c0, c1 in chunks:
        gate = _dot(h, wi_ref[:, c0:c1])
        up = _dot(h, wi_ref[:, hid + c0:hid + c1])
        act = (gate * jax.nn.sigmoid(gate) * up).astype(BF16)
        y = y + _dot(act, wo_ref[c0:c1, :])
    if final_norm:
        y = _rms(y, fg_ref[...])
    o_ref[...] = y


def _ffn(x2, g, w_in, w_out, final_g, final_norm, layer):
    t, d = x2.shape
    hid = w_out.shape[1]
    tm = _tile(t, 512)
    assert hid % MXU_DIM == 0
    step = 3 * MXU_DIM
    chunks = tuple((c, min(c + step, hid)) for c in range(0, hid, step))
    resident = pl.Buffered(1)
    return pl.pallas_call(
        functools.partial(_ffn_kernel, final_norm=final_norm, chunks=chunks),
        out_shape=jax.ShapeDtypeStruct((t, d), F32),
        grid=(t // tm,),
        in_specs=[
            pl.BlockSpec((tm, d), lambda i: (i, 0)),
            pl.BlockSpec((1, d), lambda i: (0, 0)),
            pl.BlockSpec((None,) + w_in.shape[1:], lambda i: (layer, 0, 0), pipeline_mode=resident),
            pl.BlockSpec((None,) + w_out.shape[1:], lambda i: (layer, 0, 0), pipeline_mode=resident),
            pl.BlockSpec((1, d), lambda i: (0, 0)),
        ],
        out_specs=pl.BlockSpec((tm, d), lambda i: (i, 0)),
        compiler_params=_cparams("parallel"),
        name="ffn",
    )(x2, g, w_in, w_out, final_g)


def _alibi_slopes(n_heads):
    start = 2.0 ** (-8.0 / n_heads)
    return np.array([start ** (i + 1) for i in range(n_heads)], dtype=np.float32)


def kernel(x, mem, mix_norm_g, w_in, da_lambda_q1, da_lambda_k1, da_lambda_q2, da_lambda_k2, da_subln_g, ssm_conv_w, ssm_conv_b, ssm_dt_bias, ssm_A_log, ssm_D, ssm_norm_g, mem_norm_g, w_mem_kv, w_branch, w_out, ffn_norm_g, w_ffn_in, w_ffn_out, final_norm_g):
    b, s, d = x.shape
    depth = w_in.shape[0]
    t = b * s
    m = mem.shape[1]
    da_cols = DA_HEADS * 2 * DA_HEAD_DIM
    inner = SSM_HEADS * SSM_HEAD_DIM
    conv_ch = inner + 2 * SSM_GROUPS * SSM_STATE
    n_dt = 2 * SSM_HEADS
    xw = d

    o_q, o_k, o_v = 0, da_cols, 2 * da_cols
    o_z = 3 * da_cols
    o_xbc = o_z + inner
    o_dt = o_xbc + conv_ch
    o_xq = o_dt + n_dt
    o_gt = o_xq + xw
    assert n_dt <= LANES and o_dt % LANES == 0
    assert o_q == 0
    w_head, w_tail, w_dt = _w_in_prep(w_in, o_dt, n_dt, da_cols, DA_HEAD_DIM ** -0.5 * LOG2E)
    c_q, c_k, c_v, c_z = o_q, o_k, o_v, o_z
    c_xq = o_xbc
    c_gt = c_xq + xw

    w_kv_b = w_mem_kv.astype(BF16)
    w_branch_b = w_branch.astype(BF16)
    w_out_b = w_out.astype(BF16)
    w_ffn_in_b = w_ffn_in.astype(BF16)
    w_ffn_out_b = w_ffn_out.astype(BF16)

    pad_lanes = lambda v: jnp.pad(v.reshape(depth, 1, n_dt), ((0, 0), (0, 0), (0, LANES - n_dt)))
    dt_bias_p = pad_lanes(ssm_dt_bias.astype(F32))
    a_log_p = pad_lanes(ssm_A_log.astype(F32))
    d_skip = jnp.repeat(ssm_D.astype(F32), SSM_HEAD_DIM, axis=1).reshape(depth, 1, inner)
    slopes = jnp.asarray(_alibi_slopes(DA_HEADS))
    conv_b3 = ssm_conv_b.reshape(depth, 1, conv_ch)

    kv_all = _mem_kv(mem.reshape(b * m, d), mem_norm_g.reshape(depth, 1, d), w_kv_b)

    x2 = x.reshape(t, d)
    for i in range(depth):
        lambda_init = 0.8 - 0.6 * math.exp(-0.3 * i)
        lam = (jnp.exp(jnp.sum(da_lambda_q1[i] * da_lambda_k1[i]).astype(F32))
               - jnp.exp(jnp.sum(da_lambda_q2[i] * da_lambda_k2[i]).astype(F32))
               + lambda_init).reshape(1)

        proj2, xbc2, dt_raw = _in_proj(x2, mix_norm_g[i].reshape(1, d), w_head, w_tail, w_dt,
                                       ssm_conv_w, conv_b3, i, seq_len=s, conv_col0=o_xbc)
        proj3 = proj2.reshape(b, s, -1)
        xbc3 = xbc2.reshape(b, s, conv_ch)
        dt3 = dt_raw.reshape(b, s, LANES)

        y_bwd = _ssd(xbc3, 0, dt3, dt_bias_p[i], a_log_p[i], backward=True)
        y_ssm = _ssd(xbc3, 0, dt3, dt_bias_p[i], a_log_p[i], backward=False,
                     fused=(y_bwd, proj3, c_z, d_skip[i], ssm_norm_g[i].reshape(1, inner)))

        y_da = _diff_attn(proj3, slopes, lam, da_subln_g[i].reshape(1, 2 * DA_HEAD_DIM),
                          c_q, c_k, c_v, lambda_init)
        y_xa = _xattn(proj3, kv_all, c_xq, xw, i)

        x2 = _merge(y_da.reshape(t, -1), y_ssm.reshape(t, inner), y_xa.reshape(t, xw),
                    proj2, c_gt, x2, w_branch_b, w_out_b, i)
        x2 = _ffn(x2, ffn_norm_g[i].reshape(1, d), w_ffn_in_b, w_ffn_out_b,
                  final_norm_g.reshape(1, d), final_norm=(i == depth - 1), layer=i)
    return x2.reshape(b, s, d)
```

```python
import functools
import math

import numpy as np
import jax
import jax.numpy as jnp
from jax import lax
from jax.experimental import pallas as pl
from jax.experimental.pallas import tpu as pltpu

F32 = jnp.float32
BF16 = jnp.bfloat16

EPS = 1e-6
LOG2E = math.log2(math.e)
LANES = 128
MXU_DIM = 256
F32_SUBLANES = 8
BF16_SUBLANES = 16
DA_HEADS = 8
DA_HEAD_DIM = 64
DA_KEY_CHUNK = 256
SSM_HEADS = 32
SSM_HEAD_DIM = 64
SSM_GROUPS = 4
SSM_STATE = 128
SSM_CHUNK = 128
SSM_CONV = 5
IN_PROJ_STEPS = 12
XA_HEADS = 4
VMEM_LIMIT = 56 * 1024 * 1024


def _cparams(*sem):
    return pltpu.CompilerParams(dimension_semantics=sem, vmem_limit_bytes=VMEM_LIMIT)


def _rms(x, g):
    ms = jnp.mean(x * x, axis=-1, keepdims=True)
    return x * lax.rsqrt(ms + EPS) * g


def _nt_dot(a, b):
    return lax.dot_general(a, b, (((1,), (1,)), ((), ())), preferred_element_type=F32)


def _dot(a, b):
    return jnp.dot(a, b, preferred_element_type=F32)


def _tile(n, want):
    t = min(n, want)
    assert n % t == 0, (n, t)
    return t


def _w_in_prep_kernel(a_ref, b_ref, wp_ref, wc_ref, wdt_ref, *, n_a, n_c, q_blocks, q_scale, shift):
    j = pl.program_id(1)

    @pl.when(j < n_a)
    def _():
        wp_ref[...] = (a_ref[...] * jnp.where(j < q_blocks, q_scale, 1.0)).astype(wp_ref.dtype)

    @pl.when(jnp.logical_and(j >= n_a, j < n_a + n_c))
    def _():
        wc_ref[...] = a_ref[...].astype(wc_ref.dtype)

    @pl.when(j == n_a + n_c - 1)
    def _():
        wdt_ref[...] = b_ref[...].astype(wdt_ref.dtype)

    @pl.when(j >= n_a + n_c)
    def _():
        wp_ref[...] = jnp.concatenate([a_ref[shift:, :], b_ref[:shift, :]], axis=0).astype(wp_ref.dtype)


def _w_in_prep(w_in_t, n_a_rows, n_c_rows, n_skip, q_rows, q_scale):
    depth, n, d = w_in_t.shape
    w = 1024
    n_b_rows = n - n_a_rows - n_c_rows - n_skip
    assert n_a_rows % w == 0 and n_c_rows % w == 0 and n_b_rows % w == 0 and q_rows % w == 0
    assert 0 < n_skip < LANES and n_skip % BF16_SUBLANES == 0
    n_a, n_c, n_b = n_a_rows // w, n_c_rows // w, n_b_rows // w
    return pl.pallas_call(
        functools.partial(_w_in_prep_kernel, n_a=n_a, n_c=n_c, q_blocks=q_rows // w, q_scale=q_scale, shift=n_skip),
        out_shape=(jax.ShapeDtypeStruct((depth, n_a_rows + n_b_rows, d), BF16),
                   jax.ShapeDtypeStruct((depth, n_c_rows, d), BF16),
                   jax.ShapeDtypeStruct((depth, LANES, d), BF16)),
        grid=(depth, n_a + n_c + n_b),
        in_specs=[
            pl.BlockSpec((None, w, d), lambda l, j: (l, j, 0)),
            pl.BlockSpec((None, LANES, d), lambda l, j: (l, (j + 1) * (w // LANES), 0)),
        ],
        out_specs=(
            pl.BlockSpec((None, w, d), lambda l, j: (l, jnp.where(j < n_a + n_c, jnp.minimum(j, n_a - 1), j - n_c), 0)),
            pl.BlockSpec((None, w, d), lambda l, j: (l, jnp.clip(j - n_a, 0, n_c - 1), 0)),
            pl.BlockSpec((None, LANES, d), lambda l, j: (l, 0, 0)),
        ),
        compiler_params=_cparams("parallel", "arbitrary"),
        name="w_in_prep",
    )(w_in_t, w_in_t)


def _conv_silu(y, w_ref, b_ref, cols):
    s_len = y.shape[0]
    pad = SSM_CONV // 2
    assert pad <= F32_SUBLANES
    edge = jnp.zeros((F32_SUBLANES, y.shape[1]), F32)
    yp = jnp.concatenate([edge, y, edge], axis=0)
    n = yp.shape[0]
    w = lambda k: w_ref[k:k + 1, cols]
    later = earlier = None
    for d in range(pad, 0, -1):
        nxt, prv = yp * w(pad + d), yp * w(pad - d)
        later = pltpu.roll(nxt if later is None else later + nxt, n - 1, axis=0)
        earlier = pltpu.roll(prv if earlier is None else earlier + prv, 1, axis=0)
    acc = y * w(pad) + b_ref[:, cols] + (later + earlier)[F32_SUBLANES:F32_SUBLANES + s_len]
    return acc * jax.nn.sigmoid(acc)


def _in_proj_kernel(x_ref, g_ref, wp_ref, wc_ref, wdt_ref, cw_ref, cb_ref, o_ref, oc_ref, dt_ref, h_ref):
    @pl.when(pl.program_id(1) == 0)
    def _():
        h = _rms(x_ref[...], g_ref[...]).astype(BF16)
        h_ref[...] = h
        dt_ref[...] = _nt_dot(h, wdt_ref[...])

    h = h_ref[...]
    y = _nt_dot(h, wc_ref[...])
    oc_ref[...] = _conv_silu(y, cw_ref, cb_ref, slice(None)).astype(oc_ref.dtype)
    o_ref[...] = _nt_dot(h, wp_ref[...]).astype(o_ref.dtype)


def _in_proj(x2, g, w_plain, w_conv, w_dt, conv_w, conv_b, layer, *, seq_len, steps):
    t, d = x2.shape
    n_plain, n_conv = w_plain.shape[1], w_conv.shape[1]
    tm = seq_len
    tp, tc = n_plain // steps, n_conv // steps
    assert t % tm == 0 and n_plain % steps == 0 and n_conv % steps == 0 and tp % LANES == 0 and tc % LANES == 0
    return pl.pallas_call(
        _in_proj_kernel,
        out_shape=(jax.ShapeDtypeStruct((t, n_plain), BF16), jax.ShapeDtypeStruct((t, n_conv), BF16),
                   jax.ShapeDtypeStruct((t, LANES), F32)),
        grid=(t // tm, steps),
        in_specs=[
            pl.BlockSpec((tm, d), lambda i, j: (i, 0), pipeline_mode=pl.Buffered(1)),
            pl.BlockSpec((1, d), lambda i, j: (0, 0)),
            pl.BlockSpec((None, tp, d), lambda i, j: (layer, j, 0)),
            pl.BlockSpec((None, tc, d), lambda i, j: (layer, j, 0)),
            pl.BlockSpec((None, LANES, d), lambda i, j: (layer, 0, 0)),
            pl.BlockSpec((None, SSM_CONV, tc), lambda i, j: (layer, 0, j)),
            pl.BlockSpec((None, 1, tc), lambda i, j: (layer, 0, j)),
        ],
        out_specs=(
            pl.BlockSpec((tm, tp), lambda i, j: (i, j)),
            pl.BlockSpec((tm, tc), lambda i, j: (i, j)),
            pl.BlockSpec((tm, LANES), lambda i, j: (i, 0)),
        ),
        scratch_shapes=[pltpu.VMEM((tm, d), BF16)],
        compiler_params=_cparams("parallel", "arbitrary"),
        name="in_proj",
    )(x2, g, w_plain, w_conv, w_dt, conv_w, conv_b)


def _softplus(x):
    return jnp.maximum(x, 0.0) + jnp.log(1.0 + jnp.exp(-jnp.abs(x)))


def _ssd_kernel(*refs, backward, fuse):
    if fuse:
        xs0, xs1, b_ref, c_ref, dt_ref, dtb_ref, alog_ref, yb_ref, z0, z1, dsk_ref, ng_ref, o_ref, st_ref = refs
    else:
        xs0, xs1, b_ref, c_ref, dt_ref, dtb_ref, alog_ref, o_ref, st_ref = refs
    q = SSM_CHUNK

    def cols(halves, sl):
        hw = halves[0].shape[2]
        k = sl.start // hw
        return halves[k][0, :, sl.start - k * hw:sl.stop - k * hw]

    @pl.when(pl.program_id(1) == 0)
    def _():
        st_ref[...] = jnp.zeros_like(st_ref)

    dt = _softplus(dt_ref[0] + dtb_ref[...])
    a = dt * (-LOG2E * jnp.exp(alog_ref[...]))
    row = lax.broadcasted_iota(jnp.int32, (q, LANES), 0)
    cs = a
    k = 1
    while k < q:
        if backward:
            cs = cs + jnp.where(row < q - k, pltpu.roll(cs, q - k, axis=0), 0.0)
        else:
            cs = cs + jnp.where(row >= k, pltpu.roll(cs, k, axis=0), 0.0)
        k *= 2
    cs_t = cs.T
    dt_t = dt.T
    end = 0 if backward else q - 1
    w_state_t = dt_t * jnp.exp2(cs_t[:, end:end + 1] - cs_t)

    li = lax.broadcasted_iota(jnp.int32, (q, q), 0)
    si = lax.broadcasted_iota(jnp.int32, (q, q), 1)
    tri = (li <= si) if backward else (li >= si)
    lo = lax.broadcasted_iota(jnp.int32, (q, LANES), 1) < SSM_HEAD_DIM
    lane0 = SSM_HEADS if backward else 0
    pairs_per_group = SSM_HEADS // SSM_GROUPS // 2
    gw = SSM_HEADS // SSM_GROUPS * SSM_HEAD_DIM

    def blockdiag(v):
        zero = jnp.zeros_like(v)
        return jnp.concatenate([jnp.where(lo, v, zero), jnp.where(lo, zero, v)], axis=0)

    src_t = cs_t - jnp.log2(dt_t)

    for g in range(SSM_GROUPS):
        gs = slice(g * gw, (g + 1) * gw)
        bg = b_ref[0, :, g * SSM_STATE:(g + 1) * SSM_STATE]
        cg = c_ref[0, :, g * SSM_STATE:(g + 1) * SSM_STATE]
        cb = _nt_dot(cg, bg)
        bg_t = bg.astype(F32).T
        y_in = _dot(cg, st_ref[:, gs].astype(BF16))
        ys = []
        for pp in range(pairs_per_group):
            p = g * pairs_per_group + pp
            sl = slice(p * LANES, (p + 1) * LANES)
            xbd = blockdiag(cols((xs0, xs1), sl))
            m_l, b_l, e_in = [], [], []
            for h in (2 * p, 2 * p + 1):
                r = lane0 + h
                col = jnp.broadcast_to(cs[:, r:r + 1], (q, q))
                m_l.append((cb * jnp.exp2(jnp.where(tri, col - src_t[r:r + 1, :], -jnp.inf))).astype(BF16))
                b_l.append((bg_t * w_state_t[r:r + 1, :]).astype(BF16))
                e_in.append(jnp.exp2(col))
            e_pair = jnp.where(lo, e_in[0], e_in[1])
            y = _dot(jnp.concatenate(m_l, axis=1), xbd) + y_in[:, pp * LANES:(pp + 1) * LANES] * e_pair
            st_new = _dot(jnp.concatenate(b_l, axis=1), xbd)
            st_ref[:, sl] = st_ref[:, sl] * e_pair[end:end + 1, :] + st_new
            if fuse:
                ys.append(y)
            else:
                o_ref[0, :, sl] = y
        if fuse:
            yg = (jnp.concatenate(ys, axis=1) + yb_ref[0, :, gs]
                  + cols((xs0, xs1), gs).astype(F32) * dsk_ref[:, gs])
            zg = cols((z0, z1), gs).astype(F32)
            yg = yg * (zg * jax.nn.sigmoid(zg))
            o_ref[0, :, gs] = _rms(yg, ng_ref[:, gs]).astype(o_ref.dtype)


def _ssd(xbc, col_xbc, dt_raw, dt_bias, a_log, *, backward, fused=None):
    b, s, _ = xbc.shape
    q = SSM_CHUNK
    nc = s // q
    inner = SSM_HEADS * SSM_HEAD_DIM
    hw = inner // 2
    bc = SSM_GROUPS * SSM_STATE
    assert col_xbc % hw == 0 and (col_xbc + inner) % bc == 0
    cix = (lambda c: nc - 1 - c) if backward else (lambda c: c)

    def halves(col):
        return [pl.BlockSpec((1, q, hw), lambda i, c, k=k: (i, cix(c), col // hw + k)) for k in range(2)]

    in_specs = halves(col_xbc) + [
        pl.BlockSpec((1, q, bc), lambda i, c: (i, cix(c), (col_xbc + inner) // bc)),
        pl.BlockSpec((1, q, bc), lambda i, c: (i, cix(c), (col_xbc + inner) // bc + 1)),
        pl.BlockSpec((1, q, LANES), lambda i, c: (i, cix(c), 0)),
        pl.BlockSpec((1, LANES), lambda i, c: (0, 0)),
        pl.BlockSpec((1, LANES), lambda i, c: (0, 0)),
    ]
    args = [xbc, xbc, xbc, xbc, dt_raw, dt_bias, a_log]
    if fused is not None:
        y_bwd, z_src, col_z, d_skip, norm_g = fused
        assert col_z % hw == 0
        in_specs += [pl.BlockSpec((1, q, inner), lambda i, c: (i, cix(c), 0))] + halves(col_z) + [
            pl.BlockSpec((1, inner), lambda i, c: (0, 0)),
            pl.BlockSpec((1, inner), lambda i, c: (0, 0)),
        ]
        args += [y_bwd, z_src, z_src, d_skip, norm_g]
    return pl.pallas_call(
        functools.partial(_ssd_kernel, backward=backward, fuse=fused is not None),
        out_shape=jax.ShapeDtypeStruct((b, s, inner), BF16 if fused is not None else F32),
        grid=(b, nc),
        in_specs=in_specs,
        out_specs=pl.BlockSpec((1, q, inner), lambda i, c: (i, cix(c), 0)),
        scratch_shapes=[pltpu.VMEM((SSM_STATE, inner), F32)],
        compiler_params=_cparams("parallel", "arbitrary"),
        name="ssd_bwd" if backward else "ssd_fwd",
    )(*args)


def _diff_attn_kernel(slopes_ref, lam_ref, q_ref, k_ref, v_ref, g_ref, o_ref, bias_ref, vext_ref, s_ref, m_ref, a_ref,
                      *, out_scale):
    tq = q_ref.shape[1] // 2
    s_len = k_ref.shape[1]
    w = v_ref.shape[2]
    npb = s_len // (2 * tq)
    h, g = pl.program_id(0), pl.program_id(1)
    n_pairs = pl.num_programs(1) - 1
    ga = jnp.minimum(g, n_pairs - 1)
    ba, pa = ga // npb, ga % npb
    bb = jnp.maximum(g - 1, 0) // npb

    @pl.when(g == 0)
    def _():
        s_ref[...] = jnp.zeros_like(s_ref)
        m_ref[...] = jnp.zeros_like(m_ref)
        a_ref[...] = jnp.ones_like(a_ref)

    @pl.when(jnp.logical_and(ba == 0, g < n_pairs))
    def _():
        kpos = lax.broadcasted_iota(jnp.int32, (s_len, tq), 0)
        for half in range(2):
            qpos = (2 * pa + half) * tq + lax.broadcasted_iota(jnp.int32, (s_len, tq), 1)
            bias_ref[2 * pa + half] = (slopes_ref[h] * LOG2E) * jnp.abs(qpos - kpos).astype(F32)

    @pl.when(jnp.logical_and(pa == 0, g < n_pairs))
    def _():
        vext_ref[ba % 2, :w, :] = v_ref[0].astype(F32).T.astype(BF16)
        vext_ref[ba % 2, w:, :] = jnp.ones((vext_ref.shape[1] - w, s_len), BF16)

    def raw_scores(half):
        q = q_ref[0, half * tq:(half + 1) * tq, :]
        lo = lax.broadcasted_iota(jnp.int32, q.shape, 1) < DA_HEAD_DIM
        zero = jnp.zeros_like(q)
        return _nt_dot(k_ref[0], jnp.concatenate([jnp.where(lo, q, zero), jnp.where(lo, zero, q)], axis=0))

    def biased(raw, half):
        bias = bias_ref[2 * pa + half]
        s = jnp.concatenate([raw[:, :tq] - bias, raw[:, tq:] - bias], axis=1)
        return s, jnp.max(s, axis=0, keepdims=True)

    def epilogue(a):
        a1, a2 = a[:, :tq], a[:, tq:]
        o_t = a1[:w] * (1.0 / a1[w:w + 1]) - a2[:w] * (lam_ref[0] / a2[w:w + 1])
        return (_rms(o_t.T, g_ref[...]) * out_scale).astype(o_ref.dtype)

    raw0 = raw_scores(0)
    p_prev = jnp.exp2(s_ref[...] - m_ref[...]).astype(BF16)
    o_ref[0, :tq, :] = epilogue(a_ref[...])
    a_prev = _dot(vext_ref[bb % 2], p_prev)
    s0, m0 = biased(raw0, 0)
    raw1 = raw_scores(1)
    p0 = jnp.exp2(s0 - m0).astype(BF16)
    o_ref[0, tq:, :] = epilogue(a_prev)
    a_ref[...] = _dot(vext_ref[ba % 2], p0)
    s1, m1 = biased(raw1, 1)
    s_ref[...] = s1
    m_ref[...] = m1


def _diff_attn(proj3, slopes, lam, sub_g, col_q, col_k, col_v, lambda_init):
    b, s, _ = proj3.shape
    w = 2 * DA_HEAD_DIM
    tq = _tile(s, 512) // 2
    npb = s // (2 * tq)
    n_pairs = b * npb
    smem = pl.BlockSpec(memory_space=pltpu.SMEM)

    def pair_in(g):
        ga = jnp.minimum(g, n_pairs - 1)
        return ga // npb, ga % npb

    def pair_out(g):
        gb = jnp.maximum(g - 1, 0)
        return gb // npb, gb % npb

    return pl.pallas_call(
        functools.partial(_diff_attn_kernel, out_scale=1.0 - lambda_init),
        out_shape=jax.ShapeDtypeStruct((b, s, DA_HEADS * w), BF16),
        grid=(DA_HEADS, n_pairs + 1),
        in_specs=[
            smem, smem,
            pl.BlockSpec((1, 2 * tq, w), lambda h, g: (*pair_in(g), col_q // w + h)),
            pl.BlockSpec((1, s, w), lambda h, g: (pair_in(g)[0], 0, col_k // w + h)),
            pl.BlockSpec((1, s, w), lambda h, g: (pair_in(g)[0], 0, col_v // w + h)),
            pl.BlockSpec((1, w), lambda h, g: (0, 0)),
        ],
        out_specs=pl.BlockSpec((1, 2 * tq, w), lambda h, g: (*pair_out(g), h)),
        scratch_shapes=[
            pltpu.VMEM((s // tq, s, tq), F32),
            pltpu.VMEM((2, w + BF16_SUBLANES, s), BF16),
            pltpu.VMEM((s, 2 * tq), F32),
            pltpu.VMEM((1, 2 * tq), F32),
            pltpu.VMEM((w + BF16_SUBLANES, 2 * tq), F32),
        ],
        compiler_params=_cparams("parallel", "arbitrary"),
        name="diff_attn",
    )(slopes, lam, proj3, proj3, proj3, sub_g)


def _mem_kv_kernel(m_ref, g_ref, w_ref, o_ref):
    mn = _rms(m_ref[...], g_ref[0]).astype(BF16)
    o_ref[0] = _dot(mn, w_ref[0]).astype(o_ref.dtype)


def _mem_kv(mem2, mem_g, w_kv):
    rows, d = mem2.shape
    depth, _, n = w_kv.shape
    tn = _tile(n, 1024)
    return pl.pallas_call(
        _mem_kv_kernel,
        out_shape=jax.ShapeDtypeStruct((depth, rows, n), BF16),
        grid=(depth, n // tn),
        in_specs=[
            pl.BlockSpec((rows, d), lambda l, j: (0, 0)),
            pl.BlockSpec((1, 1, d), lambda l, j: (l, 0, 0)),
            pl.BlockSpec((1, d, tn), lambda l, j: (l, 0, j)),
        ],
        out_specs=pl.BlockSpec((1, rows, tn), lambda l, j: (l, 0, j)),
        compiler_params=_cparams("parallel", "parallel"),
        name="mem_kv",
    )(mem2, mem_g, w_kv)


def _xattn_kernel(q_ref, kv_ref, o_ref):
    xw = q_ref.shape[2]
    hd = xw // XA_HEADS
    for h in range(XA_HEADS):
        sl = slice(h * hd, (h + 1) * hd)
        qh = q_ref[0, :, sl] * (hd ** -0.5)
        s = _nt_dot(qh, kv_ref[0, :, sl])
        p = jnp.exp(s - jnp.max(s, axis=-1, keepdims=True))
        p = p * (1.0 / jnp.sum(p, axis=-1, keepdims=True))
        o_ref[0, :, sl] = _dot(p.astype(BF16), kv_ref[0, :, xw + h * hd:xw + (h + 1) * hd]).astype(o_ref.dtype)


def _xattn(proj3, kv, col_q, xw, layer):
    b, s, _ = proj3.shape
    m = kv.shape[1] // b
    tq = _tile(s, 512)
    return pl.pallas_call(
        _xattn_kernel,
        out_shape=jax.ShapeDtypeStruct((b, s, xw), BF16),
        grid=(b, s // tq),
        in_specs=[
            pl.BlockSpec((1, tq, xw), lambda i, t: (i, t, col_q // xw)),
            pl.BlockSpec((1, m, 2 * xw), lambda i, t: (layer, i, 0)),
        ],
        out_specs=pl.BlockSpec((1, tq, xw), lambda i, t: (i, t, 0)),
        compiler_params=_cparams("parallel", "parallel"),
        name="mem_xattn",
    )(proj3, kv)


def _merge_kernel(yda_ref, yssm_ref, yxa_ref, gt_ref, x_ref, wb_ref, wo_ref, o_ref):
    d = x_ref.shape[1]
    n_da, n_ssm = yda_ref.shape[1], yssm_ref.shape[1]
    gate = jax.nn.sigmoid(gt_ref[...].astype(F32))
    merged = (gate[:, :d] * _dot(yda_ref[...], wb_ref[:n_da, :])
              + gate[:, d:2 * d] * _dot(yssm_ref[...], wb_ref[n_da:n_da + n_ssm, :])
              + gate[:, 2 * d:] * _dot(yxa_ref[...], wb_ref[n_da + n_ssm:, :]))
    o_ref[...] = x_ref[...] + _dot(merged.astype(BF16), wo_ref[...])


def _merge(y_da, y_ssm, y_xa, proj2, col_gates, x2, w_branch, w_out, layer):
    t, d = x2.shape
    tm = _tile(t, 512)
    assert col_gates % (3 * d) == 0
    row = lambda i: (i, 0)
    whole = lambda i: (layer, 0, 0)
    return pl.pallas_call(
        _merge_kernel,
        out_shape=jax.ShapeDtypeStruct((t, d), F32),
        grid=(t // tm,),
        in_specs=[
            pl.BlockSpec((tm, y_da.shape[1]), row),
            pl.BlockSpec((tm, y_ssm.shape[1]), row),
            pl.BlockSpec((tm, y_xa.shape[1]), row),
            pl.BlockSpec((tm, 3 * d), lambda i: (i, col_gates // (3 * d))),
            pl.BlockSpec((tm, d), row),
            pl.BlockSpec((None,) + w_branch.shape[1:], whole),
            pl.BlockSpec((None,) + w_out.shape[1:], whole),
        ],
        out_specs=pl.BlockSpec((tm, d), row),
        compiler_params=_cparams("parallel"),
        name="merge_out",
    )(y_da, y_ssm, y_xa, proj2, x2, w_branch, w_out)


def _ffn_kernel(x_ref, g_ref, wi_ref, wo_ref, fg_ref, o_ref, *, final_norm, chunks):
    hid = wo_ref.shape[0]
    x = x_ref[...]
    h = _rms(x, g_ref[...]).astype(BF16)
    y = x
    for c0, c1 in chunks:
        gate = _dot(h, wi_ref[:, c0:c1])
        up = _dot(h, wi_ref[:, hid + c0:hid + c1])
        act = (gate * jax.nn.sigmoid(gate) * up).astype(BF16)
        y = y + _dot(act, wo_ref[c0:c1, :])
    if final_norm:
        y = _rms(y, fg_ref[...])
    o_ref[...] = y


def _ffn(x2, g, w_in, w_out, final_g, final_norm, layer):
    t, d = x2.shape
    hid = w_out.shape[1]
    tm = _tile(t, 512)
    assert hid % MXU_DIM == 0
    step = 3 * MXU_DIM
    chunks = tuple((c, min(c + step, hid)) for c in range(0, hid, step))
    resident = pl.Buffered(1)
    return pl.pallas_call(
        functools.partial(_ffn_kernel, final_norm=final_norm, chunks=chunks),
        out_shape=jax.ShapeDtypeStruct((t, d), F32),
        grid=(t // tm,),
        in_specs=[
            pl.BlockSpec((tm, d), lambda i: (i, 0)),
            pl.BlockSpec((1, d), lambda i: (0, 0)),
            pl.BlockSpec((None,) + w_in.shape[1:], lambda i: (layer, 0, 0), pipeline_mode=resident),
            pl.BlockSpec((None,) + w_out.shape[1:], lambda i: (layer, 0, 0), pipeline_mode=resident),
            pl.BlockSpec((1, d), lambda i: (0, 0)),
        ],
        out_specs=pl.BlockSpec((tm, d), lambda i: (i, 0)),
        compiler_params=_cparams("parallel"),
        name="ffn",
    )(x2, g, w_in, w_out, final_g)


def _alibi_slopes(n_heads):
    start = 2.0 ** (-8.0 / n_heads)
    return np.array([start ** (i + 1) for i in range(n_heads)], dtype=np.float32)


def kernel(x, mem, mix_norm_g, w_in, da_lambda_q1, da_lambda_k1, da_lambda_q2, da_lambda_k2, da_subln_g, ssm_conv_w, ssm_conv_b, ssm_dt_bias, ssm_A_log, ssm_D, ssm_norm_g, mem_norm_g, w_mem_kv, w_branch, w_out, ffn_norm_g, w_ffn_in, w_ffn_out, final_norm_g):
    b, s, d = x.shape
    depth = w_in.shape[0]
    t = b * s
    m = mem.shape[1]
    da_cols = DA_HEADS * 2 * DA_HEAD_DIM
    inner = SSM_HEADS * SSM_HEAD_DIM
    conv_ch = inner + 2 * SSM_GROUPS * SSM_STATE
    n_dt = 2 * SSM_HEADS
    xw = d

    o_q, o_k, o_v = 0, da_cols, 2 * da_cols
    o_z = 3 * da_cols
    o_xbc = o_z + inner
    o_dt = o_xbc + conv_ch
    o_xq = o_dt + n_dt
    o_gt = o_xq + xw
    assert o_q == 0
    w_plain, w_conv, w_dt = _w_in_prep(jnp.swapaxes(w_in, 1, 2), o_xbc, conv_ch, n_dt, da_cols,
                                       DA_HEAD_DIM ** -0.5 * LOG2E)
    c_q, c_k, c_v, c_z = o_q, o_k, o_v, o_z
    c_xq = o_xbc
    c_gt = c_xq + xw

    w_kv_b = w_mem_kv.astype(BF16)
    w_branch_b = w_branch.astype(BF16)
    w_out_b = w_out.astype(BF16)
    w_ffn_in_b = w_ffn_in.astype(BF16)
    w_ffn_out_b = w_ffn_out.astype(BF16)

    pad_lanes = lambda v: jnp.pad(v.reshape(depth, 1, n_dt), ((0, 0), (0, 0), (0, LANES - n_dt)))
    dt_bias_p = pad_lanes(ssm_dt_bias.astype(F32))
    a_log_p = pad_lanes(ssm_A_log.astype(F32))
    d_skip = jnp.repeat(ssm_D.astype(F32), SSM_HEAD_DIM, axis=1).reshape(depth, 1, inner)
    slopes = jnp.asarray(_alibi_slopes(DA_HEADS))
    conv_b3 = ssm_conv_b.reshape(depth, 1, conv_ch)

    kv_all = _mem_kv(mem.reshape(b * m, d), mem_norm_g.reshape(depth, 1, d), w_kv_b)

    x2 = x.reshape(t, d)
    for i in range(depth):
        lambda_init = 0.8 - 0.6 * math.exp(-0.3 * i)
        lam = (jnp.exp(jnp.sum(da_lambda_q1[i] * da_lambda_k1[i]).astype(F32))
               - jnp.exp(jnp.sum(da_lambda_q2[i] * da_lambda_k2[i]).astype(F32))
               + lambda_init).reshape(1)

        proj2, xbc2, dt_raw = _in_proj(x2, mix_norm_g[i].reshape(1, d), w_plain, w_conv, w_dt,
                                       ssm_conv_w, conv_b3, i, seq_len=s, steps=IN_PROJ_STEPS)
        proj3 = proj2.reshape(b, s, -1)
        xbc3 = xbc2.reshape(b, s, conv_ch)
        dt3 = dt_raw.reshape(b, s, LANES)

        y_bwd = _ssd(xbc3, 0, dt3, dt_bias_p[i], a_log_p[i], backward=True)
        y_ssm = _ssd(xbc3, 0, dt3, dt_bias_p[i], a_log_p[i], backward=False,
                     fused=(y_bwd, proj3, c_z, d_skip[i], ssm_norm_g[i].reshape(1, inner)))

        y_da = _diff_attn(proj3, slopes, lam, da_subln_g[i].reshape(1, 2 * DA_HEAD_DIM),
                          c_q, c_k, c_v, lambda_init)
        y_xa = _xattn(proj3, kv_all, c_xq, xw, i)

        x2 = _merge(y_da.reshape(t, -1), y_ssm.reshape(t, inner), y_xa.reshape(t, xw),
                    proj2, c_gt, x2, w_branch_b, w_out_b, i)
        x2 = _ffn(x2, ffn_norm_g[i].reshape(1, d), w_ffn_in_b, w_ffn_out_b,
                  final_norm_g.reshape(1, d), final_norm=(i == depth - 1), layer=i)
    return x2.reshape(b, s, d)
```

```python
import functools
import math

import numpy as np
import jax
import jax.numpy as jnp
from jax import lax
from jax.experimental import pallas as pl
from jax.experimental.pallas import tpu as pltpu

F32 = jnp.float32
BF16 = jnp.bfloat16

EPS = 1e-6
LOG2E = math.log2(math.e)
LANES = 128
MXU_DIM = 256
F32_SUBLANES = 8
BF16_SUBLANES = 16
DA_HEADS = 8
DA_HEAD_DIM = 64
DA_KEY_CHUNK = 256
SSM_HEADS = 32
SSM_HEAD_DIM = 64
SSM_GROUPS = 4
SSM_STATE = 128
SSM_CHUNK = 128
SSM_CONV = 5
IN_PROJ_STEPS = 12
XA_HEADS = 4
VMEM_LIMIT = 56 * 1024 * 1024


def _cparams(*sem):
    return pltpu.CompilerParams(dimension_semantics=sem, vmem_limit_bytes=VMEM_LIMIT)


def _rms(x, g):
    ms = jnp.mean(x * x, axis=-1, keepdims=True)
    return x * lax.rsqrt(ms + EPS) * g


def _nt_dot(a, b):
    return lax.dot_general(a, b, (((1,), (1,)), ((), ())), preferred_element_type=F32)


def _dot(a, b):
    return jnp.dot(a, b, preferred_element_type=F32)


def _tile(n, want):
    t = min(n, want)
    assert n % t == 0, (n, t)
    return t


def _w_in_prep_kernel(a_ref, b_ref, wp_ref, wc_ref, wdt_ref, *, n_a, n_c, q_blocks, q_scale, shift):
    j = pl.program_id(1)

    @pl.when(j < n_a)
    def _():
        wp_ref[...] = (a_ref[...] * jnp.where(j < q_blocks, q_scale, 1.0)).astype(wp_ref.dtype)

    @pl.when(jnp.logical_and(j >= n_a, j < n_a + n_c))
    def _():
        wc_ref[...] = a_ref[...].astype(wc_ref.dtype)

    @pl.when(j == n_a + n_c - 1)
    def _():
        wdt_ref[...] = b_ref[...].astype(wdt_ref.dtype)

    @pl.when(j >= n_a + n_c)
    def _():
        wp_ref[...] = jnp.concatenate([a_ref[shift:, :], b_ref[:shift, :]], axis=0).astype(wp_ref.dtype)


def _w_in_prep(w_in_t, n_a_rows, n_c_rows, n_skip, q_rows, q_scale):
    depth, n, d = w_in_t.shape
    w = 1024
    n_b_rows = n - n_a_rows - n_c_rows - n_skip
    assert n_a_rows % w == 0 and n_c_rows % w == 0 and n_b_rows % w == 0 and q_rows % w == 0
    assert 0 < n_skip < LANES and n_skip % BF16_SUBLANES == 0
    n_a, n_c, n_b = n_a_rows // w, n_c_rows // w, n_b_rows // w
    return pl.pallas_call(
        functools.partial(_w_in_prep_kernel, n_a=n_a, n_c=n_c, q_blocks=q_rows // w, q_scale=q_scale, shift=n_skip),
        out_shape=(jax.ShapeDtypeStruct((depth, n_a_rows + n_b_rows, d), BF16),
                   jax.ShapeDtypeStruct((depth, n_c_rows, d), BF16),
                   jax.ShapeDtypeStruct((depth, LANES, d), BF16)),
        grid=(depth, n_a + n_c + n_b),
        in_specs=[
            pl.BlockSpec((None, w, d), lambda l, j: (l, j, 0)),
            pl.BlockSpec((None, LANES, d), lambda l, j: (l, (j + 1) * (w // LANES), 0)),
        ],
        out_specs=(
            pl.BlockSpec((None, w, d), lambda l, j: (l, jnp.where(j < n_a + n_c, jnp.minimum(j, n_a - 1), j - n_c), 0)),
            pl.BlockSpec((None, w, d), lambda l, j: (l, jnp.clip(j - n_a, 0, n_c - 1), 0)),
            pl.BlockSpec((None, LANES, d), lambda l, j: (l, 0, 0)),
        ),
        compiler_params=_cparams("parallel", "arbitrary"),
        name="w_in_prep",
    )(w_in_t, w_in_t)


def _conv_silu(y, w_ref, b_ref, cols):
    s_len = y.shape[0]
    pad = SSM_CONV // 2
    assert pad <= F32_SUBLANES
    edge = jnp.zeros((F32_SUBLANES, y.shape[1]), F32)
    yp = jnp.concatenate([edge, y, edge], axis=0)
    n = yp.shape[0]
    w = lambda k: w_ref[k:k + 1, cols]
    later = earlier = None
    for d in range(pad, 0, -1):
        nxt, prv = yp * w(pad + d), yp * w(pad - d)
        later = pltpu.roll(nxt if later is None else later + nxt, n - 1, axis=0)
        earlier = pltpu.roll(prv if earlier is None else earlier + prv, 1, axis=0)
    acc = y * w(pad) + b_ref[:, cols] + (later + earlier)[F32_SUBLANES:F32_SUBLANES + s_len]
    return acc * jax.nn.sigmoid(acc)


def _in_proj_kernel(x_ref, g_ref, wp_ref, wc_ref, wdt_ref, cw_ref, cb_ref, o_ref, oc_ref, dt_ref, h_ref):
    @pl.when(pl.program_id(1) == 0)
    def _():
        h = _rms(x_ref[...], g_ref[...]).astype(BF16)
        h_ref[...] = h
        dt_ref[...] = _nt_dot(h, wdt_ref[...])

    h = h_ref[...]
    y = _nt_dot(h, wc_ref[...])
    oc_ref[...] = _conv_silu(y, cw_ref, cb_ref, slice(None)).astype(oc_ref.dtype)
    o_ref[...] = _nt_dot(h, wp_ref[...]).astype(o_ref.dtype)


def _in_proj(x2, g, w_plain, w_conv, w_dt, conv_w, conv_b, layer, *, seq_len, steps):
    t, d = x2.shape
    n_plain, n_conv = w_plain.shape[1], w_conv.shape[1]
    tm = seq_len
    tp, tc = n_plain // steps, n_conv // steps
    assert t % tm == 0 and n_plain % steps == 0 and n_conv % steps == 0 and tp % LANES == 0 and tc % LANES == 0
    return pl.pallas_call(
        _in_proj_kernel,
        out_shape=(jax.ShapeDtypeStruct((t, n_plain), BF16), jax.ShapeDtypeStruct((t, n_conv), BF16),
                   jax.ShapeDtypeStruct((t, LANES), F32)),
        grid=(t // tm, steps),
        in_specs=[
            pl.BlockSpec((tm, d), lambda i, j: (i, 0)),
            pl.BlockSpec((1, d), lambda i, j: (0, 0)),
            pl.BlockSpec((None, tp, d), lambda i, j: (layer, j, 0)),
            pl.BlockSpec((None, tc, d), lambda i, j: (layer, j, 0)),
            pl.BlockSpec((None, LANES, d), lambda i, j: (layer, 0, 0)),
            pl.BlockSpec((None, SSM_CONV, tc), lambda i, j: (layer, 0, j)),
            pl.BlockSpec((None, 1, tc), lambda i, j: (layer, 0, j)),
        ],
        out_specs=(
            pl.BlockSpec((tm, tp), lambda i, j: (i, j)),
            pl.BlockSpec((tm, tc), lambda i, j: (i, j)),
            pl.BlockSpec((tm, LANES), lambda i, j: (i, 0)),
        ),
        scratch_shapes=[pltpu.VMEM((tm, d), BF16)],
        compiler_params=_cparams("parallel", "arbitrary"),
        name="in_proj",
    )(x2, g, w_plain, w_conv, w_dt, conv_w, conv_b)


def _softplus(x):
    return jnp.maximum(x, 0.0) + jnp.log(1.0 + jnp.exp(-jnp.abs(x)))


def _ssd_kernel(*refs, backward, fuse):
    if fuse:
        xs0, xs1, b_ref, c_ref, dt_ref, dtb_ref, alog_ref, yb_ref, z0, z1, dsk_ref, ng_ref, o_ref, st_ref = refs
    else:
        xs0, xs1, b_ref, c_ref, dt_ref, dtb_ref, alog_ref, o_ref, st_ref = refs
    q = SSM_CHUNK

    def cols(halves, sl):
        hw = halves[0].shape[2]
        k = sl.start // hw
        return halves[k][0, :, sl.start - k * hw:sl.stop - k * hw]

    @pl.when(pl.program_id(1) == 0)
    def _():
        st_ref[...] = jnp.zeros_like(st_ref)

    dt = _softplus(dt_ref[0] + dtb_ref[...])
    a = dt * (-LOG2E * jnp.exp(alog_ref[...]))
    row = lax.broadcasted_iota(jnp.int32, (q, LANES), 0)
    cs = a
    k = 1
    while k < q:
        if backward:
            cs = cs + jnp.where(row < q - k, pltpu.roll(cs, q - k, axis=0), 0.0)
        else:
            cs = cs + jnp.where(row >= k, pltpu.roll(cs, k, axis=0), 0.0)
        k *= 2
    cs_t = cs.T
    dt_t = dt.T
    end = 0 if backward else q - 1
    w_state_t = dt_t * jnp.exp2(cs_t[:, end:end + 1] - cs_t)

    li = lax.broadcasted_iota(jnp.int32, (q, q), 0)
    si = lax.broadcasted_iota(jnp.int32, (q, q), 1)
    tri = (li <= si) if backward else (li >= si)
    lo = lax.broadcasted_iota(jnp.int32, (q, LANES), 1) < SSM_HEAD_DIM
    lane0 = SSM_HEADS if backward else 0
    pairs_per_group = SSM_HEADS // SSM_GROUPS // 2
    gw = SSM_HEADS // SSM_GROUPS * SSM_HEAD_DIM

    def blockdiag(v):
        zero = jnp.zeros_like(v)
        return jnp.concatenate([jnp.where(lo, v, zero), jnp.where(lo, zero, v)], axis=0)

    src_t = cs_t - jnp.log2(dt_t)

    for g in range(SSM_GROUPS):
        gs = slice(g * gw, (g + 1) * gw)
        bg = b_ref[0, :, g * SSM_STATE:(g + 1) * SSM_STATE]
        cg = c_ref[0, :, g * SSM_STATE:(g + 1) * SSM_STATE]
        cb = _nt_dot(cg, bg)
        bg_t = bg.astype(F32).T
        y_in = _dot(cg, st_ref[:, gs].astype(BF16))
        ys = []
        for pp in range(pairs_per_group):
            p = g * pairs_per_group + pp
            sl = slice(p * LANES, (p + 1) * LANES)
            xbd = blockdiag(cols((xs0, xs1), sl))
            m_l, b_l, e_in = [], [], []
            for h in (2 * p, 2 * p + 1):
                r = lane0 + h
                col = jnp.broadcast_to(cs[:, r:r + 1], (q, q))
                m_l.append((cb * jnp.exp2(jnp.where(tri, col - src_t[r:r + 1, :], -jnp.inf))).astype(BF16))
                b_l.append((bg_t * w_state_t[r:r + 1, :]).astype(BF16))
                e_in.append(jnp.exp2(col))
            e_pair = jnp.where(lo, e_in[0], e_in[1])
            y = _dot(jnp.concatenate(m_l, axis=1), xbd) + y_in[:, pp * LANES:(pp + 1) * LANES] * e_pair
            st_new = _dot(jnp.concatenate(b_l, axis=1), xbd)
            st_ref[:, sl] = st_ref[:, sl] * e_pair[end:end + 1, :] + st_new
            if fuse:
                ys.append(y)
            else:
                o_ref[0, :, sl] = y
        if fuse:
            yg = (jnp.concatenate(ys, axis=1) + yb_ref[0, :, gs]
                  + cols((xs0, xs1), gs).astype(F32) * dsk_ref[:, gs])
            zg = cols((z0, z1), gs).astype(F32)
            yg = yg * (zg * jax.nn.sigmoid(zg))
            o_ref[0, :, gs] = _rms(yg, ng_ref[:, gs]).astype(o_ref.dtype)


def _ssd(xbc, col_xbc, dt_raw, dt_bias, a_log, *, backward, fused=None):
    b, s, _ = xbc.shape
    q = SSM_CHUNK
    nc = s // q
    inner = SSM_HEADS * SSM_HEAD_DIM
    hw = inner // 2
    bc = SSM_GROUPS * SSM_STATE
    assert col_xbc % hw == 0 and (col_xbc + inner) % bc == 0
    cix = (lambda c: nc - 1 - c) if backward else (lambda c: c)

    def halves(col):
        return [pl.BlockSpec((1, q, hw), lambda i, c, k=k: (i, cix(c), col // hw + k)) for k in range(2)]

    in_specs = halves(col_xbc) + [
        pl.BlockSpec((1, q, bc), lambda i, c: (i, cix(c), (col_xbc + inner) // bc)),
        pl.BlockSpec((1, q, bc), lambda i, c: (i, cix(c), (col_xbc + inner) // bc + 1)),
        pl.BlockSpec((1, q, LANES), lambda i, c: (i, cix(c), 0)),
        pl.BlockSpec((1, LANES), lambda i, c: (0, 0)),
        pl.BlockSpec((1, LANES), lambda i, c: (0, 0)),
    ]
    args = [xbc, xbc, xbc, xbc, dt_raw, dt_bias, a_log]
    if fused is not None:
        y_bwd, z_src, col_z, d_skip, norm_g = fused
        assert col_z % hw == 0
        in_specs += [pl.BlockSpec((1, q, inner), lambda i, c: (i, cix(c), 0))] + halves(col_z) + [
            pl.BlockSpec((1, inner), lambda i, c: (0, 0)),
            pl.BlockSpec((1, inner), lambda i, c: (0, 0)),
        ]
        args += [y_bwd, z_src, z_src, d_skip, norm_g]
    return pl.pallas_call(
        functools.partial(_ssd_kernel, backward=backward, fuse=fused is not None),
        out_shape=jax.ShapeDtypeStruct((b, s, inner), BF16 if fused is not None else F32),
        grid=(b, nc),
        in_specs=in_specs,
        out_specs=pl.BlockSpec((1, q, inner), lambda i, c: (i, cix(c), 0)),
        scratch_shapes=[pltpu.VMEM((SSM_STATE, inner), F32)],
        compiler_params=_cparams("parallel", "arbitrary"),
        name="ssd_bwd" if backward else "ssd_fwd",
    )(*args)


def _diff_attn_kernel(slopes_ref, lam_ref, q_ref, k_ref, v_ref, g_ref, o_ref, bias_ref, vext_ref, s_ref, m_ref, a_ref,
                      *, out_scale):
    tq = q_ref.shape[1] // 2
    s_len = k_ref.shape[1]
    w = v_ref.shape[2]
    npb = s_len // (2 * tq)
    h, g = pl.program_id(0), pl.program_id(1)
    n_pairs = pl.num_programs(1) - 1
    ga = jnp.minimum(g, n_pairs - 1)
    ba, pa = ga // npb, ga % npb
    bb = jnp.maximum(g - 1, 0) // npb

    @pl.when(g == 0)
    def _():
        s_ref[...] = jnp.zeros_like(s_ref)
        m_ref[...] = jnp.zeros_like(m_ref)
        a_ref[...] = jnp.ones_like(a_ref)

    @pl.when(jnp.logical_and(ba == 0, g < n_pairs))
    def _():
        kpos = lax.broadcasted_iota(jnp.int32, (s_len, tq), 0)
        for half in range(2):
            qpos = (2 * pa + half) * tq + lax.broadcasted_iota(jnp.int32, (s_len, tq), 1)
            bias_ref[2 * pa + half] = (slopes_ref[h] * LOG2E) * jnp.abs(qpos - kpos).astype(F32)

    @pl.when(jnp.logical_and(pa == 0, g < n_pairs))
    def _():
        vext_ref[ba % 2, :w, :] = v_ref[0].astype(F32).T.astype(BF16)
        vext_ref[ba % 2, w:, :] = jnp.ones((vext_ref.shape[1] - w, s_len), BF16)

    def raw_scores(half, after=None):
        q = q_ref[0, half * tq:(half + 1) * tq, :]
        if after is not None:
            bits = pltpu.bitcast(after[:F32_SUBLANES, :w], jnp.uint32)
            zero_rows = pltpu.bitcast((bits >> 16) >> 16, F32)[:1, :].astype(q.dtype)
            q = q + zero_rows
        lo = lax.broadcasted_iota(jnp.int32, q.shape, 1) < DA_HEAD_DIM
        zero = jnp.zeros_like(q)
        return _nt_dot(k_ref[0], jnp.concatenate([jnp.where(lo, q, zero), jnp.where(lo, zero, q)], axis=0))

    def biased(raw, half):
        bias = bias_ref[2 * pa + half]
        s = jnp.concatenate([raw[:, :tq] - bias, raw[:, tq:] - bias], axis=1)
        return s, jnp.max(s, axis=0, keepdims=True)

    def epilogue(a):
        a1, a2 = a[:, :tq], a[:, tq:]
        o_t = a1[:w] * (1.0 / a1[w:w + 1]) - a2[:w] * (lam_ref[0] / a2[w:w + 1])
        return (_rms(o_t.T, g_ref[...]) * out_scale).astype(o_ref.dtype)

    raw0 = raw_scores(0)
    p_prev = jnp.exp2(s_ref[...] - m_ref[...]).astype(BF16)
    o_ref[0, :tq, :] = epilogue(a_ref[...])
    a_prev = _dot(vext_ref[bb % 2], p_prev)
    s0, m0 = biased(raw0, 0)
    raw1 = raw_scores(1, after=a_prev)
    p0 = jnp.exp2(s0 - m0).astype(BF16)
    o_ref[0, tq:, :] = epilogue(a_prev)
    a_ref[...] = _dot(vext_ref[ba % 2], p0)
    s1, m1 = biased(raw1, 1)
    s_ref[...] = s1
    m_ref[...] = m1


def _diff_attn(proj3, slopes, lam, sub_g, col_q, col_k, col_v, lambda_init):
    b, s, _ = proj3.shape
    w = 2 * DA_HEAD_DIM
    tq = _tile(s, 512) // 2
    npb = s // (2 * tq)
    n_pairs = b * npb
    smem = pl.BlockSpec(memory_space=pltpu.SMEM)

    def pair_in(g):
        ga = jnp.minimum(g, n_pairs - 1)
        return ga // npb, ga % npb

    def pair_out(g):
        gb = jnp.maximum(g - 1, 0)
        return gb // npb, gb % npb

    return pl.pallas_call(
        functools.partial(_diff_attn_kernel, out_scale=1.0 - lambda_init),
        out_shape=jax.ShapeDtypeStruct((b, s, DA_HEADS * w), BF16),
        grid=(DA_HEADS, n_pairs + 1),
        in_specs=[
            smem, smem,
            pl.BlockSpec((1, 2 * tq, w), lambda h, g: (*pair_in(g), col_q // w + h)),
            pl.BlockSpec((1, s, w), lambda h, g: (pair_in(g)[0], 0, col_k // w + h)),
            pl.BlockSpec((1, s, w), lambda h, g: (pair_in(g)[0], 0, col_v // w + h)),
            pl.BlockSpec((1, w), lambda h, g: (0, 0)),
        ],
        out_specs=pl.BlockSpec((1, 2 * tq, w), lambda h, g: (*pair_out(g), h)),
        scratch_shapes=[
            pltpu.VMEM((s // tq, s, tq), F32),
            pltpu.VMEM((2, w + BF16_SUBLANES, s), BF16),
            pltpu.VMEM((s, 2 * tq), F32),
            pltpu.VMEM((1, 2 * tq), F32),
            pltpu.VMEM((w + BF16_SUBLANES, 2 * tq), F32),
        ],
        compiler_params=_cparams("parallel", "arbitrary"),
        name="diff_attn",
    )(slopes, lam, proj3, proj3, proj3, sub_g)


def _mem_kv_kernel(m_ref, g_ref, w_ref, o_ref):
    mn = _rms(m_ref[...], g_ref[0]).astype(BF16)
    o_ref[0] = _dot(mn, w_ref[0]).astype(o_ref.dtype)


def _mem_kv(mem2, mem_g, w_kv):
    rows, d = mem2.shape
    depth, _, n = w_kv.shape
    tn = _tile(n, 1024)
    return pl.pallas_call(
        _mem_kv_kernel,
        out_shape=jax.ShapeDtypeStruct((depth, rows, n), BF16),
        grid=(depth, n // tn),
        in_specs=[
            pl.BlockSpec((rows, d), lambda l, j: (0, 0)),
            pl.BlockSpec((1, 1, d), lambda l, j: (l, 0, 0)),
            pl.BlockSpec((1, d, tn), lambda l, j: (l, 0, j)),
        ],
        out_specs=pl.BlockSpec((1, rows, tn), lambda l, j: (l, 0, j)),
        compiler_params=_cparams("parallel", "parallel"),
        name="mem_kv",
    )(mem2, mem_g, w_kv)


def _xattn_kernel(q_ref, kv_ref, o_ref):
    xw = q_ref.shape[2]
    hd = xw // XA_HEADS
    for h in range(XA_HEADS):
        sl = slice(h * hd, (h + 1) * hd)
        qh = q_ref[0, :, sl] * (hd ** -0.5)
        s = _nt_dot(qh, kv_ref[0, :, sl])
        p = jnp.exp(s - jnp.max(s, axis=-1, keepdims=True))
        p = p * (1.0 / jnp.sum(p, axis=-1, keepdims=True))
        o_ref[0, :, sl] = _dot(p.astype(BF16), kv_ref[0, :, xw + h * hd:xw + (h + 1) * hd]).astype(o_ref.dtype)


def _xattn(proj3, kv, col_q, xw, layer):
    b, s, _ = proj3.shape
    m = kv.shape[1] // b
    tq = _tile(s, 512)
    return pl.pallas_call(
        _xattn_kernel,
        out_shape=jax.ShapeDtypeStruct((b, s, xw), BF16),
        grid=(b, s // tq),
        in_specs=[
            pl.BlockSpec((1, tq, xw), lambda i, t: (i, t, col_q // xw)),
            pl.BlockSpec((1, m, 2 * xw), lambda i, t: (layer, i, 0)),
        ],
        out_specs=pl.BlockSpec((1, tq, xw), lambda i, t: (i, t, 0)),
        compiler_params=_cparams("parallel", "parallel"),
        name="mem_xattn",
    )(proj3, kv)


def _merge_kernel(yda_ref, yssm_ref, yxa_ref, gt_ref, x_ref, wb_ref, wo_ref, o_ref):
    d = x_ref.shape[1]
    n_da, n_ssm = yda_ref.shape[1], yssm_ref.shape[1]
    gate = jax.nn.sigmoid(gt_ref[...].astype(F32))
    merged = (gate[:, :d] * _dot(yda_ref[...], wb_ref[:n_da, :])
              + gate[:, d:2 * d] * _dot(yssm_ref[...], wb_ref[n_da:n_da + n_ssm, :])
              + gate[:, 2 * d:] * _dot(yxa_ref[...], wb_ref[n_da + n_ssm:, :]))
    o_ref[...] = x_ref[...] + _dot(merged.astype(BF16), wo_ref[...])


def _merge(y_da, y_ssm, y_xa, proj2, col_gates, x2, w_branch, w_out, layer):
    t, d = x2.shape
    tm = _tile(t, 512)
    assert col_gates % (3 * d) == 0
    row = lambda i: (i, 0)
    whole = lambda i: (layer, 0, 0)
    return pl.pallas_call(
        _merge_kernel,
        out_shape=jax.ShapeDtypeStruct((t, d), F32),
        grid=(t // tm,),
        in_specs=[
            pl.BlockSpec((tm, y_da.shape[1]), row),
            pl.BlockSpec((tm, y_ssm.shape[1]), row),
            pl.BlockSpec((tm, y_xa.shape[1]), row),
            pl.BlockSpec((tm, 3 * d), lambda i: (i, col_gates // (3 * d))),
            pl.BlockSpec((tm, d), row),
            pl.BlockSpec((None,) + w_branch.shape[1:], whole),
            pl.BlockSpec((None,) + w_out.shape[1:], whole),
        ],
        out_specs=pl.BlockSpec((tm, d), row),
        compiler_params=_cparams("parallel"),
        name="merge_out",
    )(y_da, y_ssm, y_xa, proj2, x2, w_branch, w_out)


def _ffn_kernel(x_ref, g_ref, wi_ref, wo_ref, fg_ref, o_ref, *, final_norm, chunks):
    hid = wo_ref.shape[0]
    x = x_ref[...]
    h = _rms(x, g_ref[...]).astype(BF16)
    y = x
    for c0, c1 in chunks:
        gate = _dot(h, wi_ref[:, c0:c1])
        up = _dot(h, wi_ref[:, hid + c0:hid + c1])
        act = (gate * jax.nn.sigmoid(gate) * up).astype(BF16)
        y = y + _dot(act, wo_ref[c0:c1, :])
    if final_norm:
        y = _rms(y, fg_ref[...])
    o_ref[...] = y


def _ffn(x2, g, w_in, w_out, final_g, final_norm, layer):
    t, d = x2.shape
    hid = w_out.shape[1]
    tm = _tile(t, 512)
    assert hid % MXU_DIM == 0
    step = 3 * MXU_DIM
    chunks = tuple((c, min(c + step, hid)) for c in range(0, hid, step))
    resident = pl.Buffered(1)
    return pl.pallas_call(
        functools.partial(_ffn_kernel, final_norm=final_norm, chunks=chunks),
        out_shape=jax.ShapeDtypeStruct((t, d), F32),
        grid=(t // tm,),
        in_specs=[
            pl.BlockSpec((tm, d), lambda i: (i, 0)),
            pl.BlockSpec((1, d), lambda i: (0, 0)),
            pl.BlockSpec((None,) + w_in.shape[1:], lambda i: (layer, 0, 0), pipeline_mode=resident),
            pl.BlockSpec((None,) + w_out.shape[1:], lambda i: (layer, 0, 0), pipeline_mode=resident),
            pl.BlockSpec((1, d), lambda i: (0, 0)),
        ],
        out_specs=pl.BlockSpec((tm, d), lambda i: (i, 0)),
        compiler_params=_cparams("parallel"),
        name="ffn",
    )(x2, g, w_in, w_out, final_g)


def _alibi_slopes(n_heads):
    start = 2.0 ** (-8.0 / n_heads)
    return np.array([start ** (i + 1) for i in range(n_heads)], dtype=np.float32)


def kernel(x, mem, mix_norm_g, w_in, da_lambda_q1, da_lambda_k1, da_lambda_q2, da_lambda_k2, da_subln_g, ssm_conv_w, ssm_conv_b, ssm_dt_bias, ssm_A_log, ssm_D, ssm_norm_g, mem_norm_g, w_mem_kv, w_branch, w_out, ffn_norm_g, w_ffn_in, w_ffn_out, final_norm_g):
    b, s, d = x.shape
    depth = w_in.shape[0]
    t = b * s
    m = mem.shape[1]
    da_cols = DA_HEADS * 2 * DA_HEAD_DIM
    inner = SSM_HEADS * SSM_HEAD_DIM
    conv_ch = inner + 2 * SSM_GROUPS * SSM_STATE
    n_dt = 2 * SSM_HEADS
    xw = d

    o_q, o_k, o_v = 0, da_cols, 2 * da_cols
    o_z = 3 * da_cols
    o_xbc = o_z + inner
    o_dt = o_xbc + conv_ch
    o_xq = o_dt + n_dt
    o_gt = o_xq + xw
    assert o_q == 0
    w_plain, w_conv, w_dt = _w_in_prep(jnp.swapaxes(w_in, 1, 2), o_xbc, conv_ch, n_dt, da_cols,
                                       DA_HEAD_DIM ** -0.5 * LOG2E)
    c_q, c_k, c_v, c_z = o_q, o_k, o_v, o_z
    c_xq = o_xbc
    c_gt = c_xq + xw

    w_kv_b = w_mem_kv.astype(BF16)
    w_branch_b = w_branch.astype(BF16)
    w_out_b = w_out.astype(BF16)
    w_ffn_in_b = w_ffn_in.astype(BF16)
    w_ffn_out_b = w_ffn_out.astype(BF16)

    pad_lanes = lambda v: jnp.pad(v.reshape(depth, 1, n_dt), ((0, 0), (0, 0), (0, LANES - n_dt)))
    dt_bias_p = pad_lanes(ssm_dt_bias.astype(F32))
    a_log_p = pad_lanes(ssm_A_log.astype(F32))
    d_skip = jnp.repeat(ssm_D.astype(F32), SSM_HEAD_DIM, axis=1).reshape(depth, 1, inner)
    slopes = jnp.asarray(_alibi_slopes(DA_HEADS))
    conv_b3 = ssm_conv_b.reshape(depth, 1, conv_ch)

    kv_all = _mem_kv(mem.reshape(b * m, d), mem_norm_g.reshape(depth, 1, d), w_kv_b)

    x2 = x.reshape(t, d)
    for i in range(depth):
        lambda_init = 0.8 - 0.6 * math.exp(-0.3 * i)
        lam = (jnp.exp(jnp.sum(da_lambda_q1[i] * da_lambda_k1[i]).astype(F32))
               - jnp.exp(jnp.sum(da_lambda_q2[i] * da_lambda_k2[i]).astype(F32))
               + lambda_init).reshape(1)

        proj2, xbc2, dt_raw = _in_proj(x2, mix_norm_g[i].reshape(1, d), w_plain, w_conv, w_dt,
                                       ssm_conv_w, conv_b3, i, seq_len=s, steps=IN_PROJ_STEPS)
        proj3 = proj2.reshape(b, s, -1)
        xbc3 = xbc2.reshape(b, s, conv_ch)
        dt3 = dt_raw.reshape(b, s, LANES)

        y_bwd = _ssd(xbc3, 0, dt3, dt_bias_p[i], a_log_p[i], backward=True)
        y_ssm = _ssd(xbc3, 0, dt3, dt_bias_p[i], a_log_p[i], backward=False,
                     fused=(y_bwd, proj3, c_z, d_skip[i], ssm_norm_g[i].reshape(1, inner)))

        y_da = _diff_attn(proj3, slopes, lam, da_subln_g[i].reshape(1, 2 * DA_HEAD_DIM),
                          c_q, c_k, c_v, lambda_init)
        y_xa = _xattn(proj3, kv_all, c_xq, xw, i)

        x2 = _merge(y_da.reshape(t, -1), y_ssm.reshape(t, inner), y_xa.reshape(t, xw),
                    proj2, c_gt, x2, w_branch_b, w_out_b, i)
        x2 = _ffn(x2, ffn_norm_g[i].reshape(1, d), w_ffn_in_b, w_ffn_out_b,
                  final_norm_g.reshape(1, d), final_norm=(i == depth - 1), layer=i)
    return x2.reshape(b, s, d)
```

```python
import functools
import math

import numpy as np
import jax
import jax.numpy as jnp
from jax import lax
from jax.experimental import pallas as pl
from jax.experimental.pallas import tpu as pltpu

F32 = jnp.float32
BF16 = jnp.bfloat16

EPS = 1e-6
LOG2E = math.log2(math.e)
LANES = 128
MXU_DIM = 256
F32_SUBLANES = 8
BF16_SUBLANES = 16
DA_HEADS = 8
DA_HEAD_DIM = 64
DA_KEY_CHUNK = 256
SSM_HEADS = 32
SSM_HEAD_DIM = 64
SSM_GROUPS = 4
SSM_STATE = 128
SSM_CHUNK = 128
SSM_CHUNKS_PER_STEP = 4
SSM_CONV = 5
IN_PROJ_STEPS = 12
XA_HEADS = 4
VMEM_LIMIT = 56 * 1024 * 1024


def _cparams(*sem):
    return pltpu.CompilerParams(dimension_semantics=sem, vmem_limit_bytes=VMEM_LIMIT)


def _rms(x, g):
    ms = jnp.mean(x * x, axis=-1, keepdims=True)
    return x * lax.rsqrt(ms + EPS) * g


def _nt_dot(a, b):
    return lax.dot_general(a, b, (((1,), (1,)), ((), ())), preferred_element_type=F32)


def _dot(a, b):
    return jnp.dot(a, b, preferred_element_type=F32)


def _tile(n, want):
    t = min(n, want)
    assert n % t == 0, (n, t)
    return t


def _w_in_prep_kernel(a_ref, b_ref, wp_ref, wc_ref, wdt_ref, *, n_a, n_c, q_blocks, q_scale, shift):
    j = pl.program_id(1)

    @pl.when(j < n_a)
    def _():
        wp_ref[...] = (a_ref[...] * jnp.where(j < q_blocks, q_scale, 1.0)).astype(wp_ref.dtype)

    @pl.when(jnp.logical_and(j >= n_a, j < n_a + n_c))
    def _():
        wc_ref[...] = a_ref[...].astype(wc_ref.dtype)

    @pl.when(j == n_a + n_c - 1)
    def _():
        wdt_ref[...] = b_ref[...].astype(wdt_ref.dtype)

    @pl.when(j >= n_a + n_c)
    def _():
        wp_ref[...] = jnp.concatenate([a_ref[shift:, :], b_ref[:shift, :]], axis=0).astype(wp_ref.dtype)


def _w_in_prep(w_in_t, n_a_rows, n_c_rows, n_skip, q_rows, q_scale):
    depth, n, d = w_in_t.shape
    w = 1024
    n_b_rows = n - n_a_rows - n_c_rows - n_skip
    assert n_a_rows % w == 0 and n_c_rows % w == 0 and n_b_rows % w == 0 and q_rows % w == 0
    assert 0 < n_skip < LANES and n_skip % BF16_SUBLANES == 0
    n_a, n_c, n_b = n_a_rows // w, n_c_rows // w, n_b_rows // w
    return pl.pallas_call(
        functools.partial(_w_in_prep_kernel, n_a=n_a, n_c=n_c, q_blocks=q_rows // w, q_scale=q_scale, shift=n_skip),
        out_shape=(jax.ShapeDtypeStruct((depth, n_a_rows + n_b_rows, d), BF16),
                   jax.ShapeDtypeStruct((depth, n_c_rows, d), BF16),
                   jax.ShapeDtypeStruct((depth, LANES, d), BF16)),
        grid=(depth, n_a + n_c + n_b),
        in_specs=[
            pl.BlockSpec((None, w, d), lambda l, j: (l, j, 0)),
            pl.BlockSpec((None, LANES, d), lambda l, j: (l, (j + 1) * (w // LANES), 0)),
        ],
        out_specs=(
            pl.BlockSpec((None, w, d), lambda l, j: (l, jnp.where(j < n_a + n_c, jnp.minimum(j, n_a - 1), j - n_c), 0)),
            pl.BlockSpec((None, w, d), lambda l, j: (l, jnp.clip(j - n_a, 0, n_c - 1), 0)),
            pl.BlockSpec((None, LANES, d), lambda l, j: (l, 0, 0)),
        ),
        compiler_params=_cparams("parallel", "arbitrary"),
        name="w_in_prep",
    )(w_in_t, w_in_t)


def _conv_silu(y, w_ref, b_ref, cols):
    s_len = y.shape[0]
    pad = SSM_CONV // 2
    assert pad <= F32_SUBLANES
    edge = jnp.zeros((F32_SUBLANES, y.shape[1]), F32)
    yp = jnp.concatenate([edge, y, edge], axis=0)
    n = yp.shape[0]
    w = lambda k: w_ref[k:k + 1, cols]
    later = earlier = None
    for d in range(pad, 0, -1):
        nxt, prv = yp * w(pad + d), yp * w(pad - d)
        later = pltpu.roll(nxt if later is None else later + nxt, n - 1, axis=0)
        earlier = pltpu.roll(prv if earlier is None else earlier + prv, 1, axis=0)
    acc = y * w(pad) + b_ref[:, cols] + (later + earlier)[F32_SUBLANES:F32_SUBLANES + s_len]
    return acc * jax.nn.sigmoid(acc)


def _in_proj_kernel(x_ref, g_ref, wp_ref, wc_ref, wdt_ref, cw_ref, cb_ref, o_ref, oc_ref, dt_ref, h_ref):
    @pl.when(pl.program_id(1) == 0)
    def _():
        h = _rms(x_ref[...], g_ref[...]).astype(BF16)
        h_ref[...] = h
        dt_ref[...] = _nt_dot(h, wdt_ref[...])

    h = h_ref[...]
    y = _nt_dot(h, wc_ref[...])
    oc_ref[...] = _conv_silu(y, cw_ref, cb_ref, slice(None)).astype(oc_ref.dtype)
    o_ref[...] = _nt_dot(h, wp_ref[...]).astype(o_ref.dtype)


def _in_proj(x2, g, w_plain, w_conv, w_dt, conv_w, conv_b, layer, *, seq_len, steps):
    t, d = x2.shape
    n_plain, n_conv = w_plain.shape[1], w_conv.shape[1]
    tm = seq_len
    tp, tc = n_plain // steps, n_conv // steps
    assert t % tm == 0 and n_plain % steps == 0 and n_conv % steps == 0 and tp % LANES == 0 and tc % LANES == 0
    return pl.pallas_call(
        _in_proj_kernel,
        out_shape=(jax.ShapeDtypeStruct((t, n_plain), BF16), jax.ShapeDtypeStruct((t, n_conv), BF16),
                   jax.ShapeDtypeStruct((t, LANES), F32)),
        grid=(t // tm, steps),
        in_specs=[
            pl.BlockSpec((tm, d), lambda i, j: (i, 0)),
            pl.BlockSpec((1, d), lambda i, j: (0, 0)),
            pl.BlockSpec((None, tp, d), lambda i, j: (layer, j, 0)),
            pl.BlockSpec((None, tc, d), lambda i, j: (layer, j, 0)),
            pl.BlockSpec((None, LANES, d), lambda i, j: (layer, 0, 0)),
            pl.BlockSpec((None, SSM_CONV, tc), lambda i, j: (layer, 0, j)),
            pl.BlockSpec((None, 1, tc), lambda i, j: (layer, 0, j)),
        ],
        out_specs=(
            pl.BlockSpec((tm, tp), lambda i, j: (i, j)),
            pl.BlockSpec((tm, tc), lambda i, j: (i, j)),
            pl.BlockSpec((tm, LANES), lambda i, j: (i, 0)),
        ),
        scratch_shapes=[pltpu.VMEM((tm, d), BF16)],
        compiler_params=_cparams("parallel", "arbitrary"),
        name="in_proj",
    )(x2, g, w_plain, w_conv, w_dt, conv_w, conv_b)


def _softplus(x):
    return jnp.maximum(x, 0.0) + jnp.log(1.0 + jnp.exp(-jnp.abs(x)))


def _ssd_kernel(*refs, backward, fuse):
    if fuse:
        xs0, xs1, b_ref, c_ref, dt_ref, dtb_ref, alog_ref, yb_ref, z0, z1, dsk_ref, ng_ref, o_ref, st_ref = refs
    else:
        xs0, xs1, b_ref, c_ref, dt_ref, dtb_ref, alog_ref, o_ref, st_ref = refs
    q = SSM_CHUNK

    def cols(halves, rows, sl):
        hw = halves[0].shape[2]
        k = sl.start // hw
        return halves[k][0, rows, sl.start - k * hw:sl.stop - k * hw]

    @pl.when(pl.program_id(1) == 0)
    def _():
        st_ref[...] = jnp.zeros_like(st_ref)

    def one_chunk(rows):
        dt = _softplus(dt_ref[0, rows, :] + dtb_ref[...])
        a = dt * (-LOG2E * jnp.exp(alog_ref[...]))
        row = lax.broadcasted_iota(jnp.int32, (q, LANES), 0)
        cs = a
        k = 1
        while k < q:
            if backward:
                cs = cs + jnp.where(row < q - k, pltpu.roll(cs, q - k, axis=0), 0.0)
            else:
                cs = cs + jnp.where(row >= k, pltpu.roll(cs, k, axis=0), 0.0)
            k *= 2
        cs_t = cs.T
        dt_t = dt.T
        end = 0 if backward else q - 1
        w_state_t = dt_t * jnp.exp2(cs_t[:, end:end + 1] - cs_t)

        li = lax.broadcasted_iota(jnp.int32, (q, q), 0)
        si = lax.broadcasted_iota(jnp.int32, (q, q), 1)
        tri = (li <= si) if backward else (li >= si)
        lo = lax.broadcasted_iota(jnp.int32, (q, LANES), 1) < SSM_HEAD_DIM
        lane0 = SSM_HEADS if backward else 0
        pairs_per_group = SSM_HEADS // SSM_GROUPS // 2
        gw = SSM_HEADS // SSM_GROUPS * SSM_HEAD_DIM

        def blockdiag(v):
            zero = jnp.zeros_like(v)
            return jnp.concatenate([jnp.where(lo, v, zero), jnp.where(lo, zero, v)], axis=0)

        src_t = cs_t - jnp.log2(dt_t)

        for g in range(SSM_GROUPS):
            gs = slice(g * gw, (g + 1) * gw)
            bg = b_ref[0, rows, g * SSM_STATE:(g + 1) * SSM_STATE]
            cg = c_ref[0, rows, g * SSM_STATE:(g + 1) * SSM_STATE]
            cb = _nt_dot(cg, bg)
            bg_t = bg.astype(F32).T
            y_in = _dot(cg, st_ref[:, gs].astype(BF16))
            ys = []
            for pp in range(pairs_per_group):
                p = g * pairs_per_group + pp
                sl = slice(p * LANES, (p + 1) * LANES)
                xbd = blockdiag(cols((xs0, xs1), rows, sl))
                m_l, b_l, e_in = [], [], []
                for h in (2 * p, 2 * p + 1):
                    r = lane0 + h
                    col = jnp.broadcast_to(cs[:, r:r + 1], (q, q))
                    m_l.append((cb * jnp.exp2(jnp.where(tri, col - src_t[r:r + 1, :], -jnp.inf))).astype(BF16))
                    b_l.append((bg_t * w_state_t[r:r + 1, :]).astype(BF16))
                    e_in.append(jnp.exp2(col))
                e_pair = jnp.where(lo, e_in[0], e_in[1])
                y = _dot(jnp.concatenate(m_l, axis=1), xbd) + y_in[:, pp * LANES:(pp + 1) * LANES] * e_pair
                st_new = _dot(jnp.concatenate(b_l, axis=1), xbd)
                st_ref[:, sl] = st_ref[:, sl] * e_pair[end:end + 1, :] + st_new
                if fuse:
                    ys.append(y)
                else:
                    o_ref[0, rows, sl] = y
            if fuse:
                yg = (jnp.concatenate(ys, axis=1) + yb_ref[0, rows, gs]
                      + cols((xs0, xs1), rows, gs).astype(F32) * dsk_ref[:, gs])
                zg = cols((z0, z1), rows, gs).astype(F32)
                yg = yg * (zg * jax.nn.sigmoid(zg))
                o_ref[0, rows, gs] = _rms(yg, ng_ref[:, gs]).astype(o_ref.dtype)

    n_sub = dt_ref.shape[1] // q
    for sub in (reversed(range(n_sub)) if backward else range(n_sub)):
        one_chunk(slice(sub * q, (sub + 1) * q))


def _ssd(xbc, col_xbc, dt_raw, dt_bias, a_log, *, backward, fused=None):
    b, s, _ = xbc.shape
    q = SSM_CHUNK * _tile(s // SSM_CHUNK, SSM_CHUNKS_PER_STEP)
    nc = s // q
    inner = SSM_HEADS * SSM_HEAD_DIM
    hw = inner // 2
    bc = SSM_GROUPS * SSM_STATE
    assert col_xbc % hw == 0 and (col_xbc + inner) % bc == 0
    cix = (lambda c: nc - 1 - c) if backward else (lambda c: c)

    def halves(col):
        return [pl.BlockSpec((1, q, hw), lambda i, c, k=k: (i, cix(c), col // hw + k)) for k in range(2)]

    in_specs = halves(col_xbc) + [
        pl.BlockSpec((1, q, bc), lambda i, c: (i, cix(c), (col_xbc + inner) // bc)),
        pl.BlockSpec((1, q, bc), lambda i, c: (i, cix(c), (col_xbc + inner) // bc + 1)),
        pl.BlockSpec((1, q, LANES), lambda i, c: (i, cix(c), 0)),
        pl.BlockSpec((1, LANES), lambda i, c: (0, 0)),
        pl.BlockSpec((1, LANES), lambda i, c: (0, 0)),
    ]
    args = [xbc, xbc, xbc, xbc, dt_raw, dt_bias, a_log]
    if fused is not None:
        y_bwd, z_src, col_z, d_skip, norm_g = fused
        assert col_z % hw == 0
        in_specs += [pl.BlockSpec((1, q, inner), lambda i, c: (i, cix(c), 0))] + halves(col_z) + [
            pl.BlockSpec((1, inner), lambda i, c: (0, 0)),
            pl.BlockSpec((1, inner), lambda i, c: (0, 0)),
        ]
        args += [y_bwd, z_src, z_src, d_skip, norm_g]
    return pl.pallas_call(
        functools.partial(_ssd_kernel, backward=backward, fuse=fused is not None),
        out_shape=jax.ShapeDtypeStruct((b, s, inner), BF16 if fused is not None else F32),
        grid=(b, nc),
        in_specs=in_specs,
        out_specs=pl.BlockSpec((1, q, inner), lambda i, c: (i, cix(c), 0)),
        scratch_shapes=[pltpu.VMEM((SSM_STATE, inner), F32)],
        compiler_params=_cparams("parallel", "arbitrary"),
        name="ssd_bwd" if backward else "ssd_fwd",
    )(*args)


def _diff_attn_kernel(slopes_ref, lam_ref, q_ref, k_ref, v_ref, g_ref, o_ref, bias_ref, vext_ref, s_ref, m_ref, a_ref,
                      *, out_scale):
    tq = q_ref.shape[1] // 2
    s_len = k_ref.shape[1]
    w = v_ref.shape[2]
    npb = s_len // (2 * tq)
    h, g = pl.program_id(0), pl.program_id(1)
    n_pairs = pl.num_programs(1) - 1
    ga = jnp.minimum(g, n_pairs - 1)
    ba, pa = ga // npb, ga % npb
    bb = jnp.maximum(g - 1, 0) // npb

    @pl.when(g == 0)
    def _():
        s_ref[...] = jnp.zeros_like(s_ref)
        m_ref[...] = jnp.zeros_like(m_ref)
        a_ref[...] = jnp.ones_like(a_ref)

    @pl.when(jnp.logical_and(ba == 0, g < n_pairs))
    def _():
        kpos = lax.broadcasted_iota(jnp.int32, (s_len, tq), 0)
        for half in range(2):
            qpos = (2 * pa + half) * tq + lax.broadcasted_iota(jnp.int32, (s_len, tq), 1)
            bias_ref[2 * pa + half] = (slopes_ref[h] * LOG2E) * jnp.abs(qpos - kpos).astype(F32)

    @pl.when(jnp.logical_and(pa == 0, g < n_pairs))
    def _():
        vext_ref[ba % 2, :w, :] = v_ref[0].astype(F32).T.astype(BF16)
        vext_ref[ba % 2, w:, :] = jnp.ones((vext_ref.shape[1] - w, s_len), BF16)

    def raw_scores(half, after=None):
        q = q_ref[0, half * tq:(half + 1) * tq, :]
        if after is not None:
            bits = pltpu.bitcast(after[:F32_SUBLANES, :w], jnp.uint32)
            zero_rows = pltpu.bitcast((bits >> 16) >> 16, F32)[:1, :].astype(q.dtype)
            q = q + zero_rows
        lo = lax.broadcasted_iota(jnp.int32, q.shape, 1) < DA_HEAD_DIM
        zero = jnp.zeros_like(q)
        return _nt_dot(k_ref[0], jnp.concatenate([jnp.where(lo, q, zero), jnp.where(lo, zero, q)], axis=0))

    def biased(raw, half):
        bias = bias_ref[2 * pa + half]
        s = jnp.concatenate([raw[:, :tq] - bias, raw[:, tq:] - bias], axis=1)
        return s, jnp.max(s, axis=0, keepdims=True)

    def epilogue(a):
        a1, a2 = a[:, :tq], a[:, tq:]
        o_t = a1[:w] * (1.0 / a1[w:w + 1]) - a2[:w] * (lam_ref[0] / a2[w:w + 1])
        return (_rms(o_t.T, g_ref[...]) * out_scale).astype(o_ref.dtype)

    raw0 = raw_scores(0)
    p_prev = jnp.exp2(s_ref[...] - m_ref[...]).astype(BF16)
    o_ref[0, :tq, :] = epilogue(a_ref[...])
    a_prev = _dot(vext_ref[bb % 2], p_prev)
    s0, m0 = biased(raw0, 0)
    raw1 = raw_scores(1, after=a_prev)
    p0 = jnp.exp2(s0 - m0).astype(BF16)
    o_ref[0, tq:, :] = epilogue(a_prev)
    a_ref[...] = _dot(vext_ref[ba % 2], p0)
    s1, m1 = biased(raw1, 1)
    s_ref[...] = s1
    m_ref[...] = m1


def _diff_attn(proj3, slopes, lam, sub_g, col_q, col_k, col_v, lambda_init):
    b, s, _ = proj3.shape
    w = 2 * DA_HEAD_DIM
    tq = _tile(s, 512) // 2
    npb = s // (2 * tq)
    n_pairs = b * npb
    smem = pl.BlockSpec(memory_space=pltpu.SMEM)

    def pair_in(g):
        ga = jnp.minimum(g, n_pairs - 1)
        return ga // npb, ga % npb

    def pair_out(g):
        gb = jnp.maximum(g - 1, 0)
        return gb // npb, gb % npb

    return pl.pallas_call(
        functools.partial(_diff_attn_kernel, out_scale=1.0 - lambda_init),
        out_shape=jax.ShapeDtypeStruct((b, s, DA_HEADS * w), BF16),
        grid=(DA_HEADS, n_pairs + 1),
        in_specs=[
            smem, smem,
            pl.BlockSpec((1, 2 * tq, w), lambda h, g: (*pair_in(g), col_q // w + h)),
            pl.BlockSpec((1, s, w), lambda h, g: (pair_in(g)[0], 0, col_k // w + h)),
            pl.BlockSpec((1, s, w), lambda h, g: (pair_in(g)[0], 0, col_v // w + h)),
            pl.BlockSpec((1, w), lambda h, g: (0, 0)),
        ],
        out_specs=pl.BlockSpec((1, 2 * tq, w), lambda h, g: (*pair_out(g), h)),
        scratch_shapes=[
            pltpu.VMEM((s // tq, s, tq), F32),
            pltpu.VMEM((2, w + BF16_SUBLANES, s), BF16),
            pltpu.VMEM((s, 2 * tq), F32),
            pltpu.VMEM((1, 2 * tq), F32),
            pltpu.VMEM((w + BF16_SUBLANES, 2 * tq), F32),
        ],
        compiler_params=_cparams("parallel", "arbitrary"),
        name="diff_attn",
    )(slopes, lam, proj3, proj3, proj3, sub_g)


def _mem_kv_kernel(m_ref, g_ref, w_ref, o_ref):
    mn = _rms(m_ref[...], g_ref[0]).astype(BF16)
    o_ref[0] = _dot(mn, w_ref[0]).astype(o_ref.dtype)


def _mem_kv(mem2, mem_g, w_kv):
    rows, d = mem2.shape
    depth, _, n = w_kv.shape
    tn = _tile(n, 1024)
    return pl.pallas_call(
        _mem_kv_kernel,
        out_shape=jax.ShapeDtypeStruct((depth, rows, n), BF16),
        grid=(depth, n // tn),
        in_specs=[
            pl.BlockSpec((rows, d), lambda l, j: (0, 0)),
            pl.BlockSpec((1, 1, d), lambda l, j: (l, 0, 0)),
            pl.BlockSpec((1, d, tn), lambda l, j: (l, 0, j)),
        ],
        out_specs=pl.BlockSpec((1, rows, tn), lambda l, j: (l, 0, j)),
        compiler_params=_cparams("parallel", "parallel"),
        name="mem_kv",
    )(mem2, mem_g, w_kv)


def _xattn_kernel(q_ref, kv_ref, o_ref):
    xw = q_ref.shape[2]
    hd = xw // XA_HEADS
    for h in range(XA_HEADS):
        sl = slice(h * hd, (h + 1) * hd)
        qh = q_ref[0, :, sl] * (hd ** -0.5)
        s = _nt_dot(qh, kv_ref[0, :, sl])
        p = jnp.exp(s - jnp.max(s, axis=-1, keepdims=True))
        p = p * (1.0 / jnp.sum(p, axis=-1, keepdims=True))
        o_ref[0, :, sl] = _dot(p.astype(BF16), kv_ref[0, :, xw + h * hd:xw + (h + 1) * hd]).astype(o_ref.dtype)


def _xattn(proj3, kv, col_q, xw, layer):
    b, s, _ = proj3.shape
    m = kv.shape[1] // b
    tq = _tile(s, 512)
    return pl.pallas_call(
        _xattn_kernel,
        out_shape=jax.ShapeDtypeStruct((b, s, xw), BF16),
        grid=(b, s // tq),
        in_specs=[
            pl.BlockSpec((1, tq, xw), lambda i, t: (i, t, col_q // xw)),
            pl.BlockSpec((1, m, 2 * xw), lambda i, t: (layer, i, 0)),
        ],
        out_specs=pl.BlockSpec((1, tq, xw), lambda i, t: (i, t, 0)),
        compiler_params=_cparams("parallel", "parallel"),
        name="mem_xattn",
    )(proj3, kv)


def _merge_kernel(yda_ref, yssm_ref, yxa_ref, gt_ref, x_ref, wb_ref, wo_ref, o_ref):
    d = x_ref.shape[1]
    n_da, n_ssm = yda_ref.shape[1], yssm_ref.shape[1]
    gate = jax.nn.sigmoid(gt_ref[...].astype(F32))
    merged = (gate[:, :d] * _dot(yda_ref[...], wb_ref[:n_da, :])
              + gate[:, d:2 * d] * _dot(yssm_ref[...], wb_ref[n_da:n_da + n_ssm, :])
              + gate[:, 2 * d:] * _dot(yxa_ref[...], wb_ref[n_da + n_ssm:, :]))
    o_ref[...] = x_ref[...] + _dot(merged.astype(BF16), wo_ref[...])


def _merge(y_da, y_ssm, y_xa, proj2, col_gates, x2, w_branch, w_out, layer):
    t, d = x2.shape
    tm = _tile(t, 512)
    assert col_gates % (3 * d) == 0
    row = lambda i: (i, 0)
    whole = lambda i: (layer, 0, 0)
    return pl.pallas_call(
        _merge_kernel,
        out_shape=jax.ShapeDtypeStruct((t, d), F32),
        grid=(t // tm,),
        in_specs=[
            pl.BlockSpec((tm, y_da.shape[1]), row),
            pl.BlockSpec((tm, y_ssm.shape[1]), row),
            pl.BlockSpec((tm, y_xa.shape[1]), row),
            pl.BlockSpec((tm, 3 * d), lambda i: (i, col_gates // (3 * d))),
            pl.BlockSpec((tm, d), row),
            pl.BlockSpec((None,) + w_branch.shape[1:], whole),
            pl.BlockSpec((None,) + w_out.shape[1:], whole),
        ],
        out_specs=pl.BlockSpec((tm, d), row),
        compiler_params=_cparams("parallel"),
        name="merge_out",
    )(y_da, y_ssm, y_xa, proj2, x2, w_branch, w_out)


def _ffn_kernel(x_ref, g_ref, wi_ref, wo_ref, fg_ref, o_ref, *, final_norm, chunks):
    hid = wo_ref.shape[0]
    x = x_ref[...]
    h = _rms(x, g_ref[...]).astype(BF16)
    y = x
    for c0, c1 in chunks:
        gate = _dot(h, wi_ref[:, c0:c1])
        up = _dot(h, wi_ref[:, hid + c0:hid + c1])
        act = (gate * jax.nn.sigmoid(gate) * up).astype(BF16)
        y = y + _dot(act, wo_ref[c0:c1, :])
    if final_norm:
        y = _rms(y, fg_ref[...])
    o_ref[...] = y


def _ffn(x2, g, w_in, w_out, final_g, final_norm, layer):
    t, d = x2.shape
    hid = w_out.shape[1]
    tm = _tile(t, 512)
    assert hid % MXU_DIM == 0
    step = 3 * MXU_DIM
    chunks = tuple((c, min(c + step, hid)) for c in range(0, hid, step))
    resident = pl.Buffered(1)
    return pl.pallas_call(
        functools.partial(_ffn_kernel, final_norm=final_norm, chunks=chunks),
        out_shape=jax.ShapeDtypeStruct((t, d), F32),
        grid=(t // tm,),
        in_specs=[
            pl.BlockSpec((tm, d), lambda i: (i, 0)),
            pl.BlockSpec((1, d), lambda i: (0, 0)),
            pl.BlockSpec((None,) + w_in.shape[1:], lambda i: (layer, 0, 0), pipeline_mode=resident),
            pl.BlockSpec((None,) + w_out.shape[1:], lambda i: (layer, 0, 0), pipeline_mode=resident),
            pl.BlockSpec((1, d), lambda i: (0, 0)),
        ],
        out_specs=pl.BlockSpec((tm, d), lambda i: (i, 0)),
        compiler_params=_cparams("parallel"),
        name="ffn",
    )(x2, g, w_in, w_out, final_g)


def _alibi_slopes(n_heads):
    start = 2.0 ** (-8.0 / n_heads)
    return np.array([start ** (i + 1) for i in range(n_heads)], dtype=np.float32)


def kernel(x, mem, mix_norm_g, w_in, da_lambda_q1, da_lambda_k1, da_lambda_q2, da_lambda_k2, da_subln_g, ssm_conv_w, ssm_conv_b, ssm_dt_bias, ssm_A_log, ssm_D, ssm_norm_g, mem_norm_g, w_mem_kv, w_branch, w_out, ffn_norm_g, w_ffn_in, w_ffn_out, final_norm_g):
    b, s, d = x.shape
    depth = w_in.shape[0]
    t = b * s
    m = mem.shape[1]
    da_cols = DA_HEADS * 2 * DA_HEAD_DIM
    inner = SSM_HEADS * SSM_HEAD_DIM
    conv_ch = inner + 2 * SSM_GROUPS * SSM_STATE
    n_dt = 2 * SSM_HEADS
    xw = d

    o_q, o_k, o_v = 0, da_cols, 2 * da_cols
    o_z = 3 * da_cols
    o_xbc = o_z + inner
    o_dt = o_xbc + conv_ch
    o_xq = o_dt + n_dt
    o_gt = o_xq + xw
    assert o_q == 0
    w_plain, w_conv, w_dt = _w_in_prep(jnp.swapaxes(w_in, 1, 2), o_xbc, conv_ch, n_dt, da_cols,
                                       DA_HEAD_DIM ** -0.5 * LOG2E)
    c_q, c_k, c_v, c_z = o_q, o_k, o_v, o_z
    c_xq = o_xbc
    c_gt = c_xq + xw

    w_kv_b = w_mem_kv.astype(BF16)
    w_branch_b = w_branch.astype(BF16)
    w_out_b = w_out.astype(BF16)
    w_ffn_in_b = w_ffn_in.astype(BF16)
    w_ffn_out_b = w_ffn_out.astype(BF16)

    pad_lanes = lambda v: jnp.pad(v.reshape(depth, 1, n_dt), ((0, 0), (0, 0), (0, LANES - n_dt)))
    dt_bias_p = pad_lanes(ssm_dt_bias.astype(F32))
    a_log_p = pad_lanes(ssm_A_log.astype(F32))
    d_skip = jnp.repeat(ssm_D.astype(F32), SSM_HEAD_DIM, axis=1).reshape(depth, 1, inner)
    slopes = jnp.asarray(_alibi_slopes(DA_HEADS))
    conv_b3 = ssm_conv_b.reshape(depth, 1, conv_ch)

    kv_all = _mem_kv(mem.reshape(b * m, d), mem_norm_g.reshape(depth, 1, d), w_kv_b)

    x2 = x.reshape(t, d)
    for i in range(depth):
        lambda_init = 0.8 - 0.6 * math.exp(-0.3 * i)
        lam = (jnp.exp(jnp.sum(da_lambda_q1[i] * da_lambda_k1[i]).astype(F32))
               - jnp.exp(jnp.sum(da_lambda_q2[i] * da_lambda_k2[i]).astype(F32))
               + lambda_init).reshape(1)

        proj2, xbc2, dt_raw = _in_proj(x2, mix_norm_g[i].reshape(1, d), w_plain, w_conv, w_dt,
                                       ssm_conv_w, conv_b3, i, seq_len=s, steps=IN_PROJ_STEPS)
        proj3 = proj2.reshape(b, s, -1)
        xbc3 = xbc2.reshape(b, s, conv_ch)
        dt3 = dt_raw.reshape(b, s, LANES)

        y_bwd = _ssd(xbc3, 0, dt3, dt_bias_p[i], a_log_p[i], backward=True)
        y_ssm = _ssd(xbc3, 0, dt3, dt_bias_p[i], a_log_p[i], backward=False,
                     fused=(y_bwd, proj3, c_z, d_skip[i], ssm_norm_g[i].reshape(1, inner)))

        y_da = _diff_attn(proj3, slopes, lam, da_subln_g[i].reshape(1, 2 * DA_HEAD_DIM),
                          c_q, c_k, c_v, lambda_init)
        y_xa = _xattn(proj3, kv_all, c_xq, xw, i)

        x2 = _merge(y_da.reshape(t, -1), y_ssm.reshape(t, inner), y_xa.reshape(t, xw),
                    proj2, c_gt, x2, w_branch_b, w_out_b, i)
        x2 = _ffn(x2, ffn_norm_g[i].reshape(1, d), w_ffn_in_b, w_ffn_out_b,
                  final_norm_g.reshape(1, d), final_norm=(i == depth - 1), layer=i)
    return x2.reshape(b, s, d)
```

```python
import functools
import math

import numpy as np
import jax
import jax.numpy as jnp
from jax import lax
from jax.experimental import pallas as pl
from jax.experimental.pallas import tpu as pltpu

F32 = jnp.float32
BF16 = jnp.bfloat16

EPS = 1e-6
LOG2E = math.log2(math.e)
LANES = 128
MXU_DIM = 256
F32_SUBLANES = 8
BF16_SUBLANES = 16
DA_HEADS = 8
DA_HEAD_DIM = 64
SSM_HEADS = 32
SSM_HEAD_DIM = 64
SSM_GROUPS = 4
SSM_STATE = 128
SSM_CHUNK = 128
SSM_CHUNKS_PER_STEP = 4
SSM_CONV = 5
IN_PROJ_STEPS = 12
XA_HEADS = 4
VMEM_LIMIT = 56 * 1024 * 1024


def _cparams(*sem):
    return pltpu.CompilerParams(dimension_semantics=sem, vmem_limit_bytes=VMEM_LIMIT)


def _rms(x, g):
    ms = jnp.mean(x * x, axis=-1, keepdims=True)
    return x * lax.rsqrt(ms + EPS) * g


def _nt_dot(a, b):
    return lax.dot_general(a, b, (((1,), (1,)), ((), ())), preferred_element_type=F32)


def _dot(a, b):
    return jnp.dot(a, b, preferred_element_type=F32)


def _tile(n, want):
    t = min(n, want)
    assert n % t == 0, (n, t)
    return t


def _w_in_prep_kernel(a_ref, b_ref, wp_ref, wc_ref, wdt_ref, *, n_a, n_c, q_blocks, q_scale, shift):
    j = pl.program_id(1)

    @pl.when(j < n_a)
    def _():
        wp_ref[...] = (a_ref[...] * jnp.where(j < q_blocks, q_scale, 1.0)).astype(wp_ref.dtype)

    @pl.when(jnp.logical_and(j >= n_a, j < n_a + n_c))
    def _():
        wc_ref[...] = a_ref[...].astype(wc_ref.dtype)

    @pl.when(j == n_a + n_c - 1)
    def _():
        wdt_ref[...] = b_ref[...].astype(wdt_ref.dtype)

    @pl.when(j >= n_a + n_c)
    def _():
        wp_ref[...] = jnp.concatenate([a_ref[shift:, :], b_ref[:shift, :]], axis=0).astype(wp_ref.dtype)


def _w_in_prep(w_in_t, n_a_rows, n_c_rows, n_skip, q_rows, q_scale):
    depth, n, d = w_in_t.shape
    w = 1024
    n_b_rows = n - n_a_rows - n_c_rows - n_skip
    assert n_a_rows % w == 0 and n_c_rows % w == 0 and n_b_rows % w == 0 and q_rows % w == 0
    assert 0 < n_skip < LANES and n_skip % BF16_SUBLANES == 0
    n_a, n_c, n_b = n_a_rows // w, n_c_rows // w, n_b_rows // w
    return pl.pallas_call(
        functools.partial(_w_in_prep_kernel, n_a=n_a, n_c=n_c, q_blocks=q_rows // w, q_scale=q_scale, shift=n_skip),
        out_shape=(jax.ShapeDtypeStruct((depth, n_a_rows + n_b_rows, d), BF16),
                   jax.ShapeDtypeStruct((depth, n_c_rows, d), BF16),
                   jax.ShapeDtypeStruct((depth, LANES, d), BF16)),
        grid=(depth, n_a + n_c + n_b),
        in_specs=[
            pl.BlockSpec((None, w, d), lambda l, j: (l, j, 0)),
            pl.BlockSpec((None, LANES, d), lambda l, j: (l, (j + 1) * (w // LANES), 0)),
        ],
        out_specs=(
            pl.BlockSpec((None, w, d), lambda l, j: (l, jnp.where(j < n_a + n_c, jnp.minimum(j, n_a - 1), j - n_c), 0)),
            pl.BlockSpec((None, w, d), lambda l, j: (l, jnp.clip(j - n_a, 0, n_c - 1), 0)),
            pl.BlockSpec((None, LANES, d), lambda l, j: (l, 0, 0)),
        ),
        compiler_params=_cparams("parallel", "arbitrary"),
        name="w_in_prep",
    )(w_in_t, w_in_t)


def _conv_silu(y, w_ref, b_ref, cols):
    s_len = y.shape[0]
    pad = SSM_CONV // 2
    assert pad <= F32_SUBLANES
    edge = jnp.zeros((F32_SUBLANES, y.shape[1]), F32)
    yp = jnp.concatenate([edge, y, edge], axis=0)
    n = yp.shape[0]
    w = lambda k: w_ref[k:k + 1, cols]
    later = earlier = None
    for d in range(pad, 0, -1):
        nxt, prv = yp * w(pad + d), yp * w(pad - d)
        later = pltpu.roll(nxt if later is None else later + nxt, n - 1, axis=0)
        earlier = pltpu.roll(prv if earlier is None else earlier + prv, 1, axis=0)
    acc = y * w(pad) + b_ref[:, cols] + (later + earlier)[F32_SUBLANES:F32_SUBLANES + s_len]
    return acc * jax.nn.sigmoid(acc)


def _in_proj_kernel(x_ref, g_ref, wp_ref, wc_ref, wdt_ref, cw_ref, cb_ref, o_ref, oc_ref, dt_ref, h_ref):
    @pl.when(pl.program_id(1) == 0)
    def _():
        h = _rms(x_ref[...], g_ref[...]).astype(BF16)
        h_ref[...] = h
        dt_ref[...] = _nt_dot(h, wdt_ref[...])

    h = h_ref[...]
    y = _nt_dot(h, wc_ref[...])
    oc_ref[...] = _conv_silu(y, cw_ref, cb_ref, slice(None)).astype(oc_ref.dtype)
    o_ref[...] = _nt_dot(h, wp_ref[...]).astype(o_ref.dtype)


def _in_proj(x2, g, w_plain, w_conv, w_dt, conv_w, conv_b, layer, *, seq_len, steps):
    t, d = x2.shape
    n_plain, n_conv = w_plain.shape[1], w_conv.shape[1]
    tm = seq_len
    tp, tc = n_plain // steps, n_conv // steps
    assert t % tm == 0 and n_plain % steps == 0 and n_conv % steps == 0 and tp % LANES == 0 and tc % LANES == 0
    return pl.pallas_call(
        _in_proj_kernel,
        out_shape=(jax.ShapeDtypeStruct((t, n_plain), BF16), jax.ShapeDtypeStruct((t, n_conv), BF16),
                   jax.ShapeDtypeStruct((t, LANES), F32)),
        grid=(t // tm, steps),
        in_specs=[
            pl.BlockSpec((tm, d), lambda i, j: (i, 0)),
            pl.BlockSpec((1, d), lambda i, j: (0, 0)),
            pl.BlockSpec((None, tp, d), lambda i, j: (layer, j, 0)),
            pl.BlockSpec((None, tc, d), lambda i, j: (layer, j, 0)),
            pl.BlockSpec((None, LANES, d), lambda i, j: (layer, 0, 0)),
            pl.BlockSpec((None, SSM_CONV, tc), lambda i, j: (layer, 0, j)),
            pl.BlockSpec((None, 1, tc), lambda i, j: (layer, 0, j)),
        ],
        out_specs=(
            pl.BlockSpec((tm, tp), lambda i, j: (i, j)),
            pl.BlockSpec((tm, tc), lambda i, j: (i, j)),
            pl.BlockSpec((tm, LANES), lambda i, j: (i, 0)),
        ),
        scratch_shapes=[pltpu.VMEM((tm, d), BF16)],
        compiler_params=_cparams("parallel", "arbitrary"),
        name="in_proj",
    )(x2, g, w_plain, w_conv, w_dt, conv_w, conv_b)


def _softplus(x):
    return jnp.maximum(x, 0.0) + jnp.log(1.0 + jnp.exp(-jnp.abs(x)))


def _ssd_kernel(*refs, backward, fuse):
    if fuse:
        xs0, xs1, b_ref, c_ref, dt_ref, dtb_ref, alog_ref, yb_ref, z0, z1, dsk_ref, ng_ref, o_ref, st_ref = refs
    else:
        xs0, xs1, b_ref, c_ref, dt_ref, dtb_ref, alog_ref, o_ref, st_ref = refs
    q = SSM_CHUNK

    def cols(halves, rows, sl):
        hw = halves[0].shape[2]
        k = sl.start // hw
        return halves[k][0, rows, sl.start - k * hw:sl.stop - k * hw]

    @pl.when(pl.program_id(1) == 0)
    def _():
        st_ref[...] = jnp.zeros_like(st_ref)

    def one_chunk(rows):
        dt = _softplus(dt_ref[0, rows, :] + dtb_ref[...])
        a = dt * (-LOG2E * jnp.exp(alog_ref[...]))
        row = lax.broadcasted_iota(jnp.int32, (q, LANES), 0)
        cs = a
        k = 1
        while k < q:
            if backward:
                cs = cs + jnp.where(row < q - k, pltpu.roll(cs, q - k, axis=0), 0.0)
            else:
                cs = cs + jnp.where(row >= k, pltpu.roll(cs, k, axis=0), 0.0)
            k *= 2
        cs_t = cs.T
        dt_t = dt.T
        end = 0 if backward else q - 1
        w_state_t = dt_t * jnp.exp2(cs_t[:, end:end + 1] - cs_t)

        li = lax.broadcasted_iota(jnp.int32, (q, q), 0)
        si = lax.broadcasted_iota(jnp.int32, (q, q), 1)
        tri = (li <= si) if backward else (li >= si)
        lo = lax.broadcasted_iota(jnp.int32, (q, LANES), 1) < SSM_HEAD_DIM
        lane0 = SSM_HEADS if backward else 0
        pairs_per_group = SSM_HEADS // SSM_GROUPS // 2
        gw = SSM_HEADS // SSM_GROUPS * SSM_HEAD_DIM

        def blockdiag(v):
            zero = jnp.zeros_like(v)
            return jnp.concatenate([jnp.where(lo, v, zero), jnp.where(lo, zero, v)], axis=0)

        src_t = cs_t - jnp.log2(dt_t)

        for g in range(SSM_GROUPS):
            gs = slice(g * gw, (g + 1) * gw)
            bg = b_ref[0, rows, g * SSM_STATE:(g + 1) * SSM_STATE]
            cg = c_ref[0, rows, g * SSM_STATE:(g + 1) * SSM_STATE]
            cb = _nt_dot(cg, bg)
            bg_t = bg.astype(F32).T
            y_in = _dot(cg, st_ref[:, gs].astype(BF16))
            ys = []
            for pp in range(pairs_per_group):
                p = g * pairs_per_group + pp
                sl = slice(p * LANES, (p + 1) * LANES)
                xbd = blockdiag(cols((xs0, xs1), rows, sl))
                m_l, b_l, e_in = [], [], []
                for h in (2 * p, 2 * p + 1):
                    r = lane0 + h
                    col = jnp.broadcast_to(cs[:, r:r + 1], (q, q))
                    m_l.append((cb * jnp.exp2(jnp.where(tri, col - src_t[r:r + 1, :], -jnp.inf))).astype(BF16))
                    b_l.append((bg_t * w_state_t[r:r + 1, :]).astype(BF16))
                    e_in.append(jnp.exp2(col))
                e_pair = jnp.where(lo, e_in[0], e_in[1])
                y = _dot(jnp.concatenate(m_l, axis=1), xbd) + y_in[:, pp * LANES:(pp + 1) * LANES] * e_pair
                st_new = _dot(jnp.concatenate(b_l, axis=1), xbd)
                st_ref[:, sl] = st_ref[:, sl] * e_pair[end:end + 1, :] + st_new
                if fuse:
                    ys.append(y)
                else:
                    o_ref[0, rows, sl] = y
            if fuse:
                yg = (jnp.concatenate(ys, axis=1) + yb_ref[0, rows, gs]
                      + cols((xs0, xs1), rows, gs).astype(F32) * dsk_ref[:, gs])
                zg = cols((z0, z1), rows, gs).astype(F32)
                yg = yg * (zg * jax.nn.sigmoid(zg))
                o_ref[0, rows, gs] = _rms(yg, ng_ref[:, gs]).astype(o_ref.dtype)

    n_sub = dt_ref.shape[1] // q
    for sub in (reversed(range(n_sub)) if backward else range(n_sub)):
        one_chunk(slice(sub * q, (sub + 1) * q))


def _ssd(xbc, col_xbc, dt_raw, dt_bias, a_log, *, backward, fused=None):
    b, s, _ = xbc.shape
    q = SSM_CHUNK * _tile(s // SSM_CHUNK, SSM_CHUNKS_PER_STEP)
    nc = s // q
    inner = SSM_HEADS * SSM_HEAD_DIM
    hw = inner // 2
    bc = SSM_GROUPS * SSM_STATE
    assert col_xbc % hw == 0 and (col_xbc + inner) % bc == 0
    cix = (lambda c: nc - 1 - c) if backward else (lambda c: c)

    def halves(col):
        return [pl.BlockSpec((1, q, hw), lambda i, c, k=k: (i, cix(c), col // hw + k)) for k in range(2)]

    in_specs = halves(col_xbc) + [
        pl.BlockSpec((1, q, bc), lambda i, c: (i, cix(c), (col_xbc + inner) // bc)),
        pl.BlockSpec((1, q, bc), lambda i, c: (i, cix(c), (col_xbc + inner) // bc + 1)),
        pl.BlockSpec((1, q, LANES), lambda i, c: (i, cix(c), 0)),
        pl.BlockSpec((1, LANES), lambda i, c: (0, 0)),
        pl.BlockSpec((1, LANES), lambda i, c: (0, 0)),
    ]
    args = [xbc, xbc, xbc, xbc, dt_raw, dt_bias, a_log]
    if fused is not None:
        y_bwd, z_src, col_z, d_skip, norm_g = fused
        assert col_z % hw == 0
        in_specs += [pl.BlockSpec((1, q, inner), lambda i, c: (i, cix(c), 0))] + halves(col_z) + [
            pl.BlockSpec((1, inner), lambda i, c: (0, 0)),
            pl.BlockSpec((1, inner), lambda i, c: (0, 0)),
        ]
        args += [y_bwd, z_src, z_src, d_skip, norm_g]
    return pl.pallas_call(
        functools.partial(_ssd_kernel, backward=backward, fuse=fused is not None),
        out_shape=jax.ShapeDtypeStruct((b, s, inner), BF16 if fused is not None else F32),
        grid=(b, nc),
        in_specs=in_specs,
        out_specs=pl.BlockSpec((1, q, inner), lambda i, c: (i, cix(c), 0)),
        scratch_shapes=[pltpu.VMEM((SSM_STATE, inner), F32)],
        compiler_params=_cparams("parallel", "arbitrary"),
        name="ssd_bwd" if backward else "ssd_fwd",
    )(*args)


def _diff_attn_kernel(slopes_ref, lam_ref, q_ref, k_ref, v_ref, g_ref, o_ref, bias_ref, vext_ref, s_ref, m_ref, a_ref,
                      *, out_scale):
    tq = q_ref.shape[1] // 2
    s_len = k_ref.shape[1]
    w = v_ref.shape[2]
    npb = s_len // (2 * tq)
    h, g = pl.program_id(0), pl.program_id(1)
    n_pairs = pl.num_programs(1) - 1
    ga = jnp.minimum(g, n_pairs - 1)
    ba, pa = ga // npb, ga % npb
    bb = jnp.maximum(g - 1, 0) // npb

    @pl.when(g == 0)
    def _():
        s_ref[...] = jnp.zeros_like(s_ref)
        m_ref[...] = jnp.zeros_like(m_ref)
        a_ref[...] = jnp.ones_like(a_ref)

    @pl.when(jnp.logical_and(ba == 0, g < n_pairs))
    def _():
        kpos = lax.broadcasted_iota(jnp.int32, (s_len, tq), 0)
        for half in range(2):
            qpos = (2 * pa + half) * tq + lax.broadcasted_iota(jnp.int32, (s_len, tq), 1)
            bias_ref[2 * pa + half] = (slopes_ref[h] * LOG2E) * jnp.abs(qpos - kpos).astype(F32)

    @pl.when(jnp.logical_and(pa == 0, g < n_pairs))
    def _():
        vext_ref[ba % 2, :w, :] = v_ref[0].astype(F32).T.astype(BF16)
        vext_ref[ba % 2, w:, :] = jnp.ones((vext_ref.shape[1] - w, s_len), BF16)

    def raw_scores(half, after=None):
        q = q_ref[0, half * tq:(half + 1) * tq, :]
        if after is not None:
            bits = pltpu.bitcast(after[:F32_SUBLANES, :w], jnp.uint32)
            zero_rows = pltpu.bitcast((bits >> 16) >> 16, F32)[:1, :].astype(q.dtype)
            q = q + zero_rows
        lo = lax.broadcasted_iota(jnp.int32, q.shape, 1) < DA_HEAD_DIM
        zero = jnp.zeros_like(q)
        return _nt_dot(k_ref[0], jnp.concatenate([jnp.where(lo, q, zero), jnp.where(lo, zero, q)], axis=0))

    def biased(raw, half):
        bias = bias_ref[2 * pa + half]
        s = jnp.concatenate([raw[:, :tq] - bias, raw[:, tq:] - bias], axis=1)
        return s, jnp.max(s, axis=0, keepdims=True)

    def epilogue(a):
        a1, a2 = a[:, :tq], a[:, tq:]
        o_t = a1[:w] * (1.0 / a1[w:w + 1]) - a2[:w] * (lam_ref[0] / a2[w:w + 1])
        return (_rms(o_t.T, g_ref[...]) * out_scale).astype(o_ref.dtype)

    raw0 = raw_scores(0)
    p_prev = jnp.exp2(s_ref[...] - m_ref[...]).astype(BF16)
    o_ref[0, :tq, :] = epilogue(a_ref[...])
    a_prev = _dot(vext_ref[bb % 2], p_prev)
    s0, m0 = biased(raw0, 0)
    raw1 = raw_scores(1, after=a_prev)
    p0 = jnp.exp2(s0 - m0).astype(BF16)
    o_ref[0, tq:, :] = epilogue(a_prev)
    a_ref[...] = _dot(vext_ref[ba % 2], p0)
    s1, m1 = biased(raw1, 1)
    s_ref[...] = s1
    m_ref[...] = m1


def _diff_attn(proj3, slopes, lam, sub_g, col_q, col_k, col_v, lambda_init):
    b, s, _ = proj3.shape
    w = 2 * DA_HEAD_DIM
    tq = _tile(s, 512) // 2
    npb = s // (2 * tq)
    n_pairs = b * npb
    smem = pl.BlockSpec(memory_space=pltpu.SMEM)

    def pair_in(g):
        ga = jnp.minimum(g, n_pairs - 1)
        return ga // npb, ga % npb

    def pair_out(g):
        gb = jnp.maximum(g - 1, 0)
        return gb // npb, gb % npb

    return pl.pallas_call(
        functools.partial(_diff_attn_kernel, out_scale=1.0 - lambda_init),
        out_shape=jax.ShapeDtypeStruct((b, s, DA_HEADS * w), BF16),
        grid=(DA_HEADS, n_pairs + 1),
        in_specs=[
            smem, smem,
            pl.BlockSpec((1, 2 * tq, w), lambda h, g: (*pair_in(g), col_q // w + h)),
            pl.BlockSpec((1, s, w), lambda h, g: (pair_in(g)[0], 0, col_k // w + h)),
            pl.BlockSpec((1, s, w), lambda h, g: (pair_in(g)[0], 0, col_v // w + h)),
            pl.BlockSpec((1, w), lambda h, g: (0, 0)),
        ],
        out_specs=pl.BlockSpec((1, 2 * tq, w), lambda h, g: (*pair_out(g), h)),
        scratch_shapes=[
            pltpu.VMEM((s // tq, s, tq), F32),
            pltpu.VMEM((2, w + BF16_SUBLANES, s), BF16),
            pltpu.VMEM((s, 2 * tq), F32),
            pltpu.VMEM((1, 2 * tq), F32),
            pltpu.VMEM((w + BF16_SUBLANES, 2 * tq), F32),
        ],
        compiler_params=_cparams("parallel", "arbitrary"),
        name="diff_attn",
    )(slopes, lam, proj3, proj3, proj3, sub_g)


def _mem_kv_kernel(m_ref, g_ref, w_ref, o_ref):
    mn = _rms(m_ref[...], g_ref[0]).astype(BF16)
    o_ref[0] = _dot(mn, w_ref[0]).astype(o_ref.dtype)


def _mem_kv(mem2, mem_g, w_kv):
    rows, d = mem2.shape
    depth, _, n = w_kv.shape
    tn = _tile(n, 1024)
    return pl.pallas_call(
        _mem_kv_kernel,
        out_shape=jax.ShapeDtypeStruct((depth, rows, n), BF16),
        grid=(depth, n // tn),
        in_specs=[
            pl.BlockSpec((rows, d), lambda l, j: (0, 0)),
            pl.BlockSpec((1, 1, d), lambda l, j: (l, 0, 0)),
            pl.BlockSpec((1, d, tn), lambda l, j: (l, 0, j)),
        ],
        out_specs=pl.BlockSpec((1, rows, tn), lambda l, j: (l, 0, j)),
        compiler_params=_cparams("parallel", "parallel"),
        name="mem_kv",
    )(mem2, mem_g, w_kv)


def _xattn_kernel(q_ref, kv_ref, o_ref):
    xw = q_ref.shape[2]
    hd = xw // XA_HEADS
    for h in range(XA_HEADS):
        sl = slice(h * hd, (h + 1) * hd)
        qh = q_ref[0, :, sl] * (hd ** -0.5)
        s = _nt_dot(qh, kv_ref[0, :, sl])
        p = jnp.exp(s - jnp.max(s, axis=-1, keepdims=True))
        p = p * (1.0 / jnp.sum(p, axis=-1, keepdims=True))
        o_ref[0, :, sl] = _dot(p.astype(BF16), kv_ref[0, :, xw + h * hd:xw + (h + 1) * hd]).astype(o_ref.dtype)


def _xattn(proj3, kv, col_q, xw, layer):
    b, s, _ = proj3.shape
    m = kv.shape[1] // b
    tq = _tile(s, 512)
    return pl.pallas_call(
        _xattn_kernel,
        out_shape=jax.ShapeDtypeStruct((b, s, xw), BF16),
        grid=(b, s // tq),
        in_specs=[
            pl.BlockSpec((1, tq, xw), lambda i, t: (i, t, col_q // xw)),
            pl.BlockSpec((1, m, 2 * xw), lambda i, t: (layer, i, 0)),
        ],
        out_specs=pl.BlockSpec((1, tq, xw), lambda i, t: (i, t, 0)),
        compiler_params=_cparams("parallel", "parallel"),
        name="mem_xattn",
    )(proj3, kv)


def _merge_kernel(yda_ref, yssm_ref, yxa_ref, gt_ref, x_ref, wb_ref, wo_ref, o_ref):
    d = x_ref.shape[1]
    n_da, n_ssm = yda_ref.shape[1], yssm_ref.shape[1]
    gate = jax.nn.sigmoid(gt_ref[...].astype(F32))
    merged = (gate[:, :d] * _dot(yda_ref[...], wb_ref[:n_da, :])
              + gate[:, d:2 * d] * _dot(yssm_ref[...], wb_ref[n_da:n_da + n_ssm, :])
              + gate[:, 2 * d:] * _dot(yxa_ref[...], wb_ref[n_da + n_ssm:, :]))
    o_ref[...] = x_ref[...] + _dot(merged.astype(BF16), wo_ref[...])


def _merge(y_da, y_ssm, y_xa, proj2, col_gates, x2, w_branch, w_out, layer):
    t, d = x2.shape
    tm = _tile(t, 512)
    assert col_gates % (3 * d) == 0
    row = lambda i: (i, 0)
    whole = lambda i: (layer, 0, 0)
    return pl.pallas_call(
        _merge_kernel,
        out_shape=jax.ShapeDtypeStruct((t, d), F32),
        grid=(t // tm,),
        in_specs=[
            pl.BlockSpec((tm, y_da.shape[1]), row),
            pl.BlockSpec((tm, y_ssm.shape[1]), row),
            pl.BlockSpec((tm, y_xa.shape[1]), row),
            pl.BlockSpec((tm, 3 * d), lambda i: (i, col_gates // (3 * d))),
            pl.BlockSpec((tm, d), row),
            pl.BlockSpec((None,) + w_branch.shape[1:], whole),
            pl.BlockSpec((None,) + w_out.shape[1:], whole),
        ],
        out_specs=pl.BlockSpec((tm, d), row),
        compiler_params=_cparams("parallel"),
        name="merge_out",
    )(y_da, y_ssm, y_xa, proj2, x2, w_branch, w_out)


def _ffn_kernel(x_ref, g_ref, wi_ref, wo_ref, fg_ref, o_ref, *, final_norm, chunks):
    hid = wo_ref.shape[0]
    x = x_ref[...]
    h = _rms(x, g_ref[...]).astype(BF16)
    y = x
    for c0, c1 in chunks:
        gate = _dot(h, wi_ref[:, c0:c1])
        up = _dot(h, wi_ref[:, hid + c0:hid + c1])
        act = (gate * jax.nn.sigmoid(gate) * up).astype(BF16)
        y = y + _dot(act, wo_ref[c0:c1, :])
    if final_norm:
        y = _rms(y, fg_ref[...])
    o_ref[...] = y


def _ffn(x2, g, w_in, w_out, final_g, final_norm, layer):
    t, d = x2.shape
    hid = w_out.shape[1]
    tm = _tile(t, 512)
    assert hid % MXU_DIM == 0
    step = 3 * MXU_DIM
    chunks = tuple((c, min(c + step, hid)) for c in range(0, hid, step))
    resident = pl.Buffered(1)
    return pl.pallas_call(
        functools.partial(_ffn_kernel, final_norm=final_norm, chunks=chunks),
        out_shape=jax.ShapeDtypeStruct((t, d), F32),
        grid=(t // tm,),
        in_specs=[
            pl.BlockSpec((tm, d), lambda i: (i, 0)),
            pl.BlockSpec((1, d), lambda i: (0, 0)),
            pl.BlockSpec((None,) + w_in.shape[1:], lambda i: (layer, 0, 0), pipeline_mode=resident),
            pl.BlockSpec((None,) + w_out.shape[1:], lambda i: (layer, 0, 0), pipeline_mode=resident),
            pl.BlockSpec((1, d), lambda i: (0, 0)),
        ],
        out_specs=pl.BlockSpec((tm, d), lambda i: (i, 0)),
        compiler_params=_cparams("parallel"),
        name="ffn",
    )(x2, g, w_in, w_out, final_g)


def _alibi_slopes(n_heads):
    start = 2.0 ** (-8.0 / n_heads)
    return np.array([start ** (i + 1) for i in range(n_heads)], dtype=np.float32)


def kernel(x, mem, mix_norm_g, w_in, da_lambda_q1, da_lambda_k1, da_lambda_q2, da_lambda_k2, da_subln_g, ssm_conv_w, ssm_conv_b, ssm_dt_bias, ssm_A_log, ssm_D, ssm_norm_g, mem_norm_g, w_mem_kv, w_branch, w_out, ffn_norm_g, w_ffn_in, w_ffn_out, final_norm_g):
    b, s, d = x.shape
    depth = w_in.shape[0]
    t = b * s
    m = mem.shape[1]
    da_cols = DA_HEADS * 2 * DA_HEAD_DIM
    inner = SSM_HEADS * SSM_HEAD_DIM
    conv_ch = inner + 2 * SSM_GROUPS * SSM_STATE
    n_dt = 2 * SSM_HEADS
    xw = d

    o_q, o_k, o_v = 0, da_cols, 2 * da_cols
    o_z = 3 * da_cols
    o_xbc = o_z + inner
    assert o_q == 0
    w_plain, w_conv, w_dt = _w_in_prep(jnp.swapaxes(w_in, 1, 2), o_xbc, conv_ch, n_dt, da_cols,
                                       DA_HEAD_DIM ** -0.5 * LOG2E)
    c_q, c_k, c_v, c_z = o_q, o_k, o_v, o_z
    c_xq = o_xbc
    c_gt = c_xq + xw

    w_kv_b = w_mem_kv.astype(BF16)
    w_branch_b = w_branch.astype(BF16)
    w_out_b = w_out.astype(BF16)
    w_ffn_in_b = w_ffn_in.astype(BF16)
    w_ffn_out_b = w_ffn_out.astype(BF16)

    pad_lanes = lambda v: jnp.pad(v.reshape(depth, 1, n_dt), ((0, 0), (0, 0), (0, LANES - n_dt)))
    dt_bias_p = pad_lanes(ssm_dt_bias.astype(F32))
    a_log_p = pad_lanes(ssm_A_log.astype(F32))
    d_skip = jnp.repeat(ssm_D.astype(F32), SSM_HEAD_DIM, axis=1).reshape(depth, 1, inner)
    slopes = jnp.asarray(_alibi_slopes(DA_HEADS))
    conv_b3 = ssm_conv_b.reshape(depth, 1, conv_ch)

    kv_all = _mem_kv(mem.reshape(b * m, d), mem_norm_g.reshape(depth, 1, d), w_kv_b)

    x2 = x.reshape(t, d)
    for i in range(depth):
        lambda_init = 0.8 - 0.6 * math.exp(-0.3 * i)
        lam = (jnp.exp(jnp.sum(da_lambda_q1[i] * da_lambda_k1[i]).astype(F32))
               - jnp.exp(jnp.sum(da_lambda_q2[i] * da_lambda_k2[i]).astype(F32))
               + lambda_init).reshape(1)

        proj2, xbc2, dt_raw = _in_proj(x2, mix_norm_g[i].reshape(1, d), w_plain, w_conv, w_dt,
                                       ssm_conv_w, conv_b3, i, seq_len=s, steps=IN_PROJ_STEPS)
        proj3 = proj2.reshape(b, s, -1)
        xbc3 = xbc2.reshape(b, s, conv_ch)
        dt3 = dt_raw.reshape(b, s, LANES)

        y_bwd = _ssd(xbc3, 0, dt3, dt_bias_p[i], a_log_p[i], backward=True)
        y_ssm = _ssd(xbc3, 0, dt3, dt_bias_p[i], a_log_p[i], backward=False,
                     fused=(y_bwd, proj3, c_z, d_skip[i], ssm_norm_g[i].reshape(1, inner)))

        y_da = _diff_attn(proj3, slopes, lam, da_subln_g[i].reshape(1, 2 * DA_HEAD_DIM),
                          c_q, c_k, c_v, lambda_init)
        y_xa = _xattn(proj3, kv_all, c_xq, xw, i)

        x2 = _merge(y_da.reshape(t, -1), y_ssm.reshape(t, inner), y_xa.reshape(t, xw),
                    proj2, c_gt, x2, w_branch_b, w_out_b, i)
        x2 = _ffn(x2, ffn_norm_g[i].reshape(1, d), w_ffn_in_b, w_ffn_out_b,
                  final_norm_g.reshape(1, d), final_norm=(i == depth - 1), layer=i)
    return x2.reshape(b, s, d)
```

```python
import functools
import math

import numpy as np
import jax
import jax.numpy as jnp
from jax import lax
from jax.experimental import pallas as pl
from jax.experimental.pallas import tpu as pltpu

F32 = jnp.float32
BF16 = jnp.bfloat16

EPS = 1e-6
LOG2E = math.log2(math.e)
LANES = 128
MXU_DIM = 256
F32_SUBLANES = 8
BF16_SUBLANES = 16
DA_HEADS = 8
DA_HEAD_DIM = 64
SSM_HEADS = 32
SSM_HEAD_DIM = 64
SSM_GROUPS = 4
SSM_STATE = 128
SSM_CHUNK = 128
SSM_CHUNKS_PER_STEP = 4
SSM_CONV = 5
IN_PROJ_STEPS = 12
XA_HEADS = 4
VMEM_LIMIT = 56 * 1024 * 1024


def _cparams(*sem):
    return pltpu.CompilerParams(dimension_semantics=sem, vmem_limit_bytes=VMEM_LIMIT)


def _rms(x, g):
    ms = jnp.mean(x * x, axis=-1, keepdims=True)
    return x * lax.rsqrt(ms + EPS) * g


def _nt_dot(a, b):
    return lax.dot_general(a, b, (((1,), (1,)), ((), ())), preferred_element_type=F32)


def _dot(a, b):
    return jnp.dot(a, b, preferred_element_type=F32)


def _tile(n, want):
    t = min(n, want)
    assert n % t == 0, (n, t)
    return t


def _w_in_prep_kernel(a_ref, b_ref, wp_ref, wc_ref, wdt_ref, *, n_a, n_c, q_blocks, q_scale, shift):
    j = pl.program_id(1)

    @pl.when(j < n_a)
    def _():
        wp_ref[...] = (a_ref[...] * jnp.where(j < q_blocks, q_scale, 1.0)).astype(wp_ref.dtype)

    @pl.when(jnp.logical_and(j >= n_a, j < n_a + n_c))
    def _():
        wc_ref[...] = a_ref[...].astype(wc_ref.dtype)

    @pl.when(j == n_a + n_c - 1)
    def _():
        wdt_ref[...] = b_ref[...].astype(wdt_ref.dtype)

    @pl.when(j >= n_a + n_c)
    def _():
        wp_ref[...] = jnp.concatenate([a_ref[shift:, :], b_ref[:shift, :]], axis=0).astype(wp_ref.dtype)


def _w_in_prep(w_in_t, n_a_rows, n_c_rows, n_skip, q_rows, q_scale):
    depth, n, d = w_in_t.shape
    w = 1024
    n_b_rows = n - n_a_rows - n_c_rows - n_skip
    assert n_a_rows % w == 0 and n_c_rows % w == 0 and n_b_rows % w == 0 and q_rows % w == 0
    assert 0 < n_skip < LANES and n_skip % BF16_SUBLANES == 0
    n_a, n_c, n_b = n_a_rows // w, n_c_rows // w, n_b_rows // w
    return pl.pallas_call(
        functools.partial(_w_in_prep_kernel, n_a=n_a, n_c=n_c, q_blocks=q_rows // w, q_scale=q_scale, shift=n_skip),
        out_shape=(jax.ShapeDtypeStruct((depth, n_a_rows + n_b_rows, d), BF16),
                   jax.ShapeDtypeStruct((depth, n_c_rows, d), BF16),
                   jax.ShapeDtypeStruct((depth, LANES, d), BF16)),
        grid=(depth, n_a + n_c + n_b),
        in_specs=[
            pl.BlockSpec((None, w, d), lambda l, j: (l, j, 0)),
            pl.BlockSpec((None, LANES, d), lambda l, j: (l, (j + 1) * (w // LANES), 0)),
        ],
        out_specs=(
            pl.BlockSpec((None, w, d), lambda l, j: (l, jnp.where(j < n_a + n_c, jnp.minimum(j, n_a - 1), j - n_c), 0)),
            pl.BlockSpec((None, w, d), lambda l, j: (l, jnp.clip(j - n_a, 0, n_c - 1), 0)),
            pl.BlockSpec((None, LANES, d), lambda l, j: (l, 0, 0)),
        ),
        compiler_params=_cparams("parallel", "arbitrary"),
        name="w_in_prep",
    )(w_in_t, w_in_t)


def _conv_silu(y, w_ref, b_ref, cols):
    s_len = y.shape[0]
    pad = SSM_CONV // 2
    assert pad <= F32_SUBLANES
    edge = jnp.zeros((F32_SUBLANES, y.shape[1]), F32)
    yp = jnp.concatenate([edge, y, edge], axis=0)
    n = yp.shape[0]
    w = lambda k: w_ref[k:k + 1, cols]
    later = earlier = None
    for d in range(pad, 0, -1):
        nxt, prv = yp * w(pad + d), yp * w(pad - d)
        later = pltpu.roll(nxt if later is None else later + nxt, n - 1, axis=0)
        earlier = pltpu.roll(prv if earlier is None else earlier + prv, 1, axis=0)
    acc = y * w(pad) + b_ref[:, cols] + (later + earlier)[F32_SUBLANES:F32_SUBLANES + s_len]
    return acc * jax.nn.sigmoid(acc)


def _in_proj_kernel(x_ref, g_ref, wp_ref, wc_ref, wdt_ref, cw_ref, cb_ref, o_ref, oc_ref, dt_ref, h_ref):
    @pl.when(pl.program_id(1) == 0)
    def _():
        h = _rms(x_ref[...], g_ref[...]).astype(BF16)
        h_ref[...] = h
        dt_ref[...] = _nt_dot(h, wdt_ref[...])

    h = h_ref[...]
    y = _nt_dot(h, wc_ref[...])
    oc_ref[...] = _conv_silu(y, cw_ref, cb_ref, slice(None)).astype(oc_ref.dtype)
    o_ref[...] = _nt_dot(h, wp_ref[...]).astype(o_ref.dtype)


def _in_proj(x2, g, w_plain, w_conv, w_dt, conv_w, conv_b, layer, *, seq_len, steps):
    t, d = x2.shape
    n_plain, n_conv = w_plain.shape[1], w_conv.shape[1]
    tm = seq_len
    tp, tc = n_plain // steps, n_conv // steps
    assert t % tm == 0 and n_plain % steps == 0 and n_conv % steps == 0 and tp % LANES == 0 and tc % LANES == 0
    return pl.pallas_call(
        _in_proj_kernel,
        out_shape=(jax.ShapeDtypeStruct((t, n_plain), BF16), jax.ShapeDtypeStruct((t, n_conv), BF16),
                   jax.ShapeDtypeStruct((t, LANES), F32)),
        grid=(t // tm, steps),
        in_specs=[
            pl.BlockSpec((tm, d), lambda i, j: (i, 0)),
            pl.BlockSpec((1, d), lambda i, j: (0, 0)),
            pl.BlockSpec((None, tp, d), lambda i, j: (layer, j, 0)),
            pl.BlockSpec((None, tc, d), lambda i, j: (layer, j, 0)),
            pl.BlockSpec((None, LANES, d), lambda i, j: (layer, 0, 0)),
            pl.BlockSpec((None, SSM_CONV, tc), lambda i, j: (layer, 0, j)),
            pl.BlockSpec((None, 1, tc), lambda i, j: (layer, 0, j)),
        ],
        out_specs=(
            pl.BlockSpec((tm, tp), lambda i, j: (i, j)),
            pl.BlockSpec((tm, tc), lambda i, j: (i, j)),
            pl.BlockSpec((tm, LANES), lambda i, j: (i, 0)),
        ),
        scratch_shapes=[pltpu.VMEM((tm, d), BF16)],
        compiler_params=_cparams("parallel", "arbitrary"),
        name="in_proj",
    )(x2, g, w_plain, w_conv, w_dt, conv_w, conv_b)


def _softplus(x):
    return jnp.maximum(x, 0.0) + jnp.log(1.0 + jnp.exp(-jnp.abs(x)))


def _ssd_kernel(*refs, backward, fuse):
    if fuse:
        xs0, xs1, b_ref, c_ref, dt_ref, dtb_ref, alog_ref, yb_ref, z0, z1, dsk_ref, ng_ref, o_ref, st_ref = refs
    else:
        xs0, xs1, b_ref, c_ref, dt_ref, dtb_ref, alog_ref, o_ref, st_ref = refs
    q = SSM_CHUNK

    def cols(halves, rows, sl):
        hw = halves[0].shape[2]
        k = sl.start // hw
        return halves[k][0, rows, sl.start - k * hw:sl.stop - k * hw]

    @pl.when(pl.program_id(1) == 0)
    def _():
        st_ref[...] = jnp.zeros_like(st_ref)

    def one_chunk(rows):
        dt = _softplus(dt_ref[0, rows, :] + dtb_ref[...])
        a = dt * (-LOG2E * jnp.exp(alog_ref[...]))
        row = lax.broadcasted_iota(jnp.int32, (q, LANES), 0)
        cs = a
        k = 1
        while k < q:
            if backward:
                cs = cs + jnp.where(row < q - k, pltpu.roll(cs, q - k, axis=0), 0.0)
            else:
                cs = cs + jnp.where(row >= k, pltpu.roll(cs, k, axis=0), 0.0)
            k *= 2
        cs_t = cs.T
        dt_t = dt.T
        end = 0 if backward else q - 1
        w_state_t = dt_t * jnp.exp2(cs_t[:, end:end + 1] - cs_t)

        li = lax.broadcasted_iota(jnp.int32, (q, q), 0)
        si = lax.broadcasted_iota(jnp.int32, (q, q), 1)
        tri = (li <= si) if backward else (li >= si)
        lo = lax.broadcasted_iota(jnp.int32, (q, LANES), 1) < SSM_HEAD_DIM
        lane0 = SSM_HEADS if backward else 0
        pairs_per_group = SSM_HEADS // SSM_GROUPS // 2
        gw = SSM_HEADS // SSM_GROUPS * SSM_HEAD_DIM

        def blockdiag(v):
            zero = jnp.zeros_like(v)
            return jnp.concatenate([jnp.where(lo, v, zero), jnp.where(lo, zero, v)], axis=0)

        src_t = cs_t - jnp.log2(dt_t)

        for g in range(SSM_GROUPS):
            gs = slice(g * gw, (g + 1) * gw)
            bg = b_ref[0, rows, g * SSM_STATE:(g + 1) * SSM_STATE]
            cg = c_ref[0, rows, g * SSM_STATE:(g + 1) * SSM_STATE]
            cb = _nt_dot(cg, bg)
            bg_t = bg.astype(F32).T
            y_in = _dot(cg, st_ref[:, gs].astype(BF16))
            ys = []
            for pp in range(pairs_per_group):
                p = g * pairs_per_group + pp
                sl = slice(p * LANES, (p + 1) * LANES)
                xbd = blockdiag(cols((xs0, xs1), rows, sl))
                m_l, b_l, e_in = [], [], []
                for h in (2 * p, 2 * p + 1):
                    r = lane0 + h
                    col = jnp.broadcast_to(cs[:, r:r + 1], (q, q))
                    m_l.append((cb * jnp.exp2(jnp.where(tri, col - src_t[r:r + 1, :], -jnp.inf))).astype(BF16))
                    b_l.append((bg_t * w_state_t[r:r + 1, :]).astype(BF16))
                    e_in.append(jnp.exp2(col))
                e_pair = jnp.where(lo, e_in[0], e_in[1])
                y = _dot(jnp.concatenate(m_l, axis=1), xbd) + y_in[:, pp * LANES:(pp + 1) * LANES] * e_pair
                st_new = _dot(jnp.concatenate(b_l, axis=1), xbd)
                st_ref[:, sl] = st_ref[:, sl] * e_pair[end:end + 1, :] + st_new
                if fuse:
                    ys.append(y)
                else:
                    o_ref[0, rows, sl] = y
            if fuse:
                yg = (jnp.concatenate(ys, axis=1) + yb_ref[0, rows, gs]
                      + cols((xs0, xs1), rows, gs).astype(F32) * dsk_ref[:, gs])
                zg = cols((z0, z1), rows, gs).astype(F32)
                yg = yg * (zg * jax.nn.sigmoid(zg))
                o_ref[0, rows, gs] = _rms(yg, ng_ref[:, gs]).astype(o_ref.dtype)

    n_sub = dt_ref.shape[1] // q
    for sub in (reversed(range(n_sub)) if backward else range(n_sub)):
        one_chunk(slice(sub * q, (sub + 1) * q))


def _ssd(xbc, col_xbc, dt_raw, dt_bias, a_log, *, backward, fused=None):
    b, s, _ = xbc.shape
    q = SSM_CHUNK * _tile(s // SSM_CHUNK, SSM_CHUNKS_PER_STEP)
    nc = s // q
    inner = SSM_HEADS * SSM_HEAD_DIM
    hw = inner // 2
    bc = SSM_GROUPS * SSM_STATE
    assert col_xbc % hw == 0 and (col_xbc + inner) % bc == 0
    cix = (lambda c: nc - 1 - c) if backward else (lambda c: c)

    def halves(col):
        return [pl.BlockSpec((1, q, hw), lambda i, c, k=k: (i, cix(c), col // hw + k)) for k in range(2)]

    in_specs = halves(col_xbc) + [
        pl.BlockSpec((1, q, bc), lambda i, c: (i, cix(c), (col_xbc + inner) // bc)),
        pl.BlockSpec((1, q, bc), lambda i, c: (i, cix(c), (col_xbc + inner) // bc + 1)),
        pl.BlockSpec((1, q, LANES), lambda i, c: (i, cix(c), 0)),
        pl.BlockSpec((1, LANES), lambda i, c: (0, 0)),
        pl.BlockSpec((1, LANES), lambda i, c: (0, 0)),
    ]
    args = [xbc, xbc, xbc, xbc, dt_raw, dt_bias, a_log]
    if fused is not None:
        y_bwd, z_src, col_z, d_skip, norm_g = fused
        assert col_z % hw == 0
        in_specs += [pl.BlockSpec((1, q, inner), lambda i, c: (i, cix(c), 0))] + halves(col_z) + [
            pl.BlockSpec((1, inner), lambda i, c: (0, 0)),
            pl.BlockSpec((1, inner), lambda i, c: (0, 0)),
        ]
        args += [y_bwd, z_src, z_src, d_skip, norm_g]
    return pl.pallas_call(
        functools.partial(_ssd_kernel, backward=backward, fuse=fused is not None),
        out_shape=jax.ShapeDtypeStruct((b, s, inner), BF16 if fused is not None else F32),
        grid=(b, nc),
        in_specs=in_specs,
        out_specs=pl.BlockSpec((1, q, inner), lambda i, c: (i, cix(c), 0)),
        scratch_shapes=[pltpu.VMEM((SSM_STATE, inner), F32)],
        compiler_params=_cparams("parallel", "arbitrary"),
        name="ssd_bwd" if backward else "ssd_fwd",
    )(*args)


def _diff_attn_kernel(slopes_ref, lam_ref, q_ref, k_ref, v_ref, g_ref, o_ref, bias_ref, vext_ref, s_ref, m_ref, a_ref,
                      *, out_scale, pairs_per_head):
    tq = q_ref.shape[1] // 2
    s_len = k_ref.shape[1]
    w = v_ref.shape[2]
    npb = s_len // (2 * tq)
    g = pl.program_id(0)
    n_pairs = pl.num_programs(0) - 1
    fa = jnp.minimum(g, n_pairs - 1)
    h, ga = fa // pairs_per_head, fa % pairs_per_head
    ba, pa = ga // npb, ga % npb
    slot_a = (fa // npb) % 2
    slot_b = (jnp.maximum(g - 1, 0) // npb) % 2

    @pl.when(g == 0)
    def _():
        s_ref[...] = jnp.zeros_like(s_ref)
        m_ref[...] = jnp.zeros_like(m_ref)
        a_ref[...] = jnp.ones_like(a_ref)

    @pl.when(jnp.logical_and(ba == 0, g < n_pairs))
    def _():
        kpos = lax.broadcasted_iota(jnp.int32, (s_len, tq), 0)
        for half in range(2):
            qpos = (2 * pa + half) * tq + lax.broadcasted_iota(jnp.int32, (s_len, tq), 1)
            bias_ref[2 * pa + half] = (slopes_ref[h] * LOG2E) * jnp.abs(qpos - kpos).astype(F32)

    @pl.when(jnp.logical_and(pa == 0, g < n_pairs))
    def _():
        vext_ref[slot_a, :w, :] = v_ref[0].astype(F32).T.astype(BF16)
        vext_ref[slot_a, w:, :] = jnp.ones((vext_ref.shape[1] - w, s_len), BF16)

    def raw_scores(half, after=None):
        q = q_ref[0, half * tq:(half + 1) * tq, :]
        if after is not None:
            bits = pltpu.bitcast(after[:F32_SUBLANES, :w], jnp.uint32)
            zero_rows = pltpu.bitcast((bits >> 16) >> 16, F32)[:1, :].astype(q.dtype)
            q = q + zero_rows
        lo = lax.broadcasted_iota(jnp.int32, q.shape, 1) < DA_HEAD_DIM
        zero = jnp.zeros_like(q)
        return _nt_dot(k_ref[0], jnp.concatenate([jnp.where(lo, q, zero), jnp.where(lo, zero, q)], axis=0))

    def biased(raw, half):
        bias = bias_ref[2 * pa + half]
        s = jnp.concatenate([raw[:, :tq] - bias, raw[:, tq:] - bias], axis=1)
        return s, jnp.max(s, axis=0, keepdims=True)

    def epilogue(a):
        a1, a2 = a[:, :tq], a[:, tq:]
        o_t = a1[:w] * (1.0 / a1[w:w + 1]) - a2[:w] * (lam_ref[0] / a2[w:w + 1])
        return (_rms(o_t.T, g_ref[...]) * out_scale).astype(o_ref.dtype)

    raw0 = raw_scores(0)
    p_prev = jnp.exp2(s_ref[...] - m_ref[...]).astype(BF16)
    o_ref[0, :tq, :] = epilogue(a_ref[...])
    a_prev = _dot(vext_ref[slot_b], p_prev)
    s0, m0 = biased(raw0, 0)
    raw1 = raw_scores(1, after=a_prev)
    p0 = jnp.exp2(s0 - m0).astype(BF16)
    o_ref[0, tq:, :] = epilogue(a_prev)
    a_ref[...] = _dot(vext_ref[slot_a], p0)
    s1, m1 = biased(raw1, 1)
    s_ref[...] = s1
    m_ref[...] = m1


def _diff_attn(proj3, slopes, lam, sub_g, col_q, col_k, col_v, lambda_init):
    b, s, _ = proj3.shape
    w = 2 * DA_HEAD_DIM
    tq = _tile(s, 512) // 2
    npb = s // (2 * tq)
    pph = b * npb
    n_pairs = DA_HEADS * pph
    smem = pl.BlockSpec(memory_space=pltpu.SMEM)

    def split(f):
        return f // pph, (f % pph) // npb, f % npb

    def pair_in(g):
        return split(jnp.minimum(g, n_pairs - 1))

    def pair_out(g):
        return split(jnp.maximum(g - 1, 0))

    def q_map(g):
        h, i, p = pair_in(g)
        return i, p, col_q // w + h

    def kv_map(col):
        def index(g):
            h, i, _ = pair_in(g)
            return i, 0, col // w + h
        return index

    def o_map(g):
        h, i, p = pair_out(g)
        return i, p, h

    return pl.pallas_call(
        functools.partial(_diff_attn_kernel, out_scale=1.0 - lambda_init, pairs_per_head=pph),
        out_shape=jax.ShapeDtypeStruct((b, s, DA_HEADS * w), BF16),
        grid=(n_pairs + 1,),
        in_specs=[
            smem, smem,
            pl.BlockSpec((1, 2 * tq, w), q_map),
            pl.BlockSpec((1, s, w), kv_map(col_k)),
            pl.BlockSpec((1, s, w), kv_map(col_v)),
            pl.BlockSpec((1, w), lambda g: (0, 0)),
        ],
        out_specs=pl.BlockSpec((1, 2 * tq, w), o_map),
        scratch_shapes=[
            pltpu.VMEM((s // tq, s, tq), F32),
            pltpu.VMEM((2, w + BF16_SUBLANES, s), BF16),
            pltpu.VMEM((s, 2 * tq), F32),
            pltpu.VMEM((1, 2 * tq), F32),
            pltpu.VMEM((w + BF16_SUBLANES, 2 * tq), F32),
        ],
        compiler_params=_cparams("arbitrary"),
        name="diff_attn",
    )(slopes, lam, proj3, proj3, proj3, sub_g)


def _mem_kv_kernel(m_ref, g_ref, w_ref, o_ref):
    mn = _rms(m_ref[...], g_ref[0]).astype(BF16)
    o_ref[0] = _dot(mn, w_ref[0]).astype(o_ref.dtype)


def _mem_kv(mem2, mem_g, w_kv):
    rows, d = mem2.shape
    depth, _, n = w_kv.shape
    tn = _tile(n, 1024)
    return pl.pallas_call(
        _mem_kv_kernel,
        out_shape=jax.ShapeDtypeStruct((depth, rows, n), BF16),
        grid=(depth, n // tn),
        in_specs=[
            pl.BlockSpec((rows, d), lambda l, j: (0, 0)),
            pl.BlockSpec((1, 1, d), lambda l, j: (l, 0, 0)),
            pl.BlockSpec((1, d, tn), lambda l, j: (l, 0, j)),
        ],
        out_specs=pl.BlockSpec((1, rows, tn), lambda l, j: (l, 0, j)),
        compiler_params=_cparams("parallel", "parallel"),
        name="mem_kv",
    )(mem2, mem_g, w_kv)


def _xattn_kernel(q_ref, kv_ref, o_ref):
    xw = q_ref.shape[2]
    hd = xw // XA_HEADS
    for h in range(XA_HEADS):
        sl = slice(h * hd, (h + 1) * hd)
        qh = q_ref[0, :, sl] * (hd ** -0.5)
        s = _nt_dot(qh, kv_ref[0, :, sl])
        p = jnp.exp(s - jnp.max(s, axis=-1, keepdims=True))
        p = p * (1.0 / jnp.sum(p, axis=-1, keepdims=True))
        o_ref[0, :, sl] = _dot(p.astype(BF16), kv_ref[0, :, xw + h * hd:xw + (h + 1) * hd]).astype(o_ref.dtype)


def _xattn(proj3, kv, col_q, xw, layer):
    b, s, _ = proj3.shape
    m = kv.shape[1] // b
    tq = _tile(s, 512)
    return pl.pallas_call(
        _xattn_kernel,
        out_shape=jax.ShapeDtypeStruct((b, s, xw), BF16),
        grid=(b, s // tq),
        in_specs=[
            pl.BlockSpec((1, tq, xw), lambda i, t: (i, t, col_q // xw)),
            pl.BlockSpec((1, m, 2 * xw), lambda i, t: (layer, i, 0)),
        ],
        out_specs=pl.BlockSpec((1, tq, xw), lambda i, t: (i, t, 0)),
        compiler_params=_cparams("parallel", "parallel"),
        name="mem_xattn",
    )(proj3, kv)


def _merge_kernel(yda_ref, yssm_ref, yxa_ref, gt_ref, x_ref, wb_ref, wo_ref, o_ref):
    d = x_ref.shape[1]
    n_da, n_ssm = yda_ref.shape[1], yssm_ref.shape[1]
    gate = jax.nn.sigmoid(gt_ref[...].astype(F32))
    merged = (gate[:, :d] * _dot(yda_ref[...], wb_ref[:n_da, :])
              + gate[:, d:2 * d] * _dot(yssm_ref[...], wb_ref[n_da:n_da + n_ssm, :])
              + gate[:, 2 * d:] * _dot(yxa_ref[...], wb_ref[n_da + n_ssm:, :]))
    o_ref[...] = x_ref[...] + _dot(merged.astype(BF16), wo_ref[...])


def _merge(y_da, y_ssm, y_xa, proj2, col_gates, x2, w_branch, w_out, layer):
    t, d = x2.shape
    tm = _tile(t, 512)
    assert col_gates % (3 * d) == 0
    row = lambda i: (i, 0)
    whole = lambda i: (layer, 0, 0)
    return pl.pallas_call(
        _merge_kernel,
        out_shape=jax.ShapeDtypeStruct((t, d), F32),
        grid=(t // tm,),
        in_specs=[
            pl.BlockSpec((tm, y_da.shape[1]), row),
            pl.BlockSpec((tm, y_ssm.shape[1]), row),
            pl.BlockSpec((tm, y_xa.shape[1]), row),
            pl.BlockSpec((tm, 3 * d), lambda i: (i, col_gates // (3 * d))),
            pl.BlockSpec((tm, d), row),
            pl.BlockSpec((None,) + w_branch.shape[1:], whole),
            pl.BlockSpec((None,) + w_out.shape[1:], whole),
        ],
        out_specs=pl.BlockSpec((tm, d), row),
        compiler_params=_cparams("parallel"),
        name="merge_out",
    )(y_da, y_ssm, y_xa, proj2, x2, w_branch, w_out)


def _ffn_kernel(x_ref, g_ref, wi_ref, wo_ref, fg_ref, o_ref, *, final_norm, chunks):
    hid = wo_ref.shape[0]
    x = x_ref[...]
    h = _rms(x, g_ref[...]).astype(BF16)
    y = x
    for c0, c1 in chunks:
        gate = _dot(h, wi_ref[:, c0:c1])
        up = _dot(h, wi_ref[:, hid + c0:hid + c1])
        act = (gate * jax.nn.sigmoid(gate) * up).astype(BF16)
        y = y + _dot(act, wo_ref[c0:c1, :])
    if final_norm:
        y = _rms(y, fg_ref[...])
    o_ref[...] = y


def _ffn(x2, g, w_in, w_out, final_g, final_norm, layer):
    t, d = x2.shape
    hid = w_out.shape[1]
    tm = _tile(t, 512)
    assert hid % MXU_DIM == 0
    step = 3 * MXU_DIM
    chunks = tuple((c, min(c + step, hid)) for c in range(0, hid, step))
    resident = pl.Buffered(1)
    return pl.pallas_call(
        functools.partial(_ffn_kernel, final_norm=final_norm, chunks=chunks),
        out_shape=jax.ShapeDtypeStruct((t, d), F32),
        grid=(t // tm,),
        in_specs=[
            pl.BlockSpec((tm, d), lambda i: (i, 0)),
            pl.BlockSpec((1, d), lambda i: (0, 0)),
            pl.BlockSpec((None,) + w_in.shape[1:], lambda i: (layer, 0, 0), pipeline_mode=resident),
            pl.BlockSpec((None,) + w_out.shape[1:], lambda i: (layer, 0, 0), pipeline_mode=resident),
            pl.BlockSpec((1, d), lambda i: (0, 0)),
        ],
        out_specs=pl.BlockSpec((tm, d), lambda i: (i, 0)),
        compiler_params=_cparams("parallel"),
        name="ffn",
    )(x2, g, w_in, w_out, final_g)


def _alibi_slopes(n_heads):
    start = 2.0 ** (-8.0 / n_heads)
    return np.array([start ** (i + 1) for i in range(n_heads)], dtype=np.float32)


def kernel(x, mem, mix_norm_g, w_in, da_lambda_q1, da_lambda_k1, da_lambda_q2, da_lambda_k2, da_subln_g, ssm_conv_w, ssm_conv_b, ssm_dt_bias, ssm_A_log, ssm_D, ssm_norm_g, mem_norm_g, w_mem_kv, w_branch, w_out, ffn_norm_g, w_ffn_in, w_ffn_out, final_norm_g):
    b, s, d = x.shape
    depth = w_in.shape[0]
    t = b * s
    m = mem.shape[1]
    da_cols = DA_HEADS * 2 * DA_HEAD_DIM
    inner = SSM_HEADS * SSM_HEAD_DIM
    conv_ch = inner + 2 * SSM_GROUPS * SSM_STATE
    n_dt = 2 * SSM_HEADS
    xw = d

    o_q, o_k, o_v = 0, da_cols, 2 * da_cols
    o_z = 3 * da_cols
    o_xbc = o_z + inner
    assert o_q == 0
    w_plain, w_conv, w_dt = _w_in_prep(jnp.swapaxes(w_in, 1, 2), o_xbc, conv_ch, n_dt, da_cols,
                                       DA_HEAD_DIM ** -0.5 * LOG2E)
    c_q, c_k, c_v, c_z = o_q, o_k, o_v, o_z
    c_xq = o_xbc
    c_gt = c_xq + xw

    w_kv_b = w_mem_kv.astype(BF16)
    w_branch_b = w_branch.astype(BF16)
    w_out_b = w_out.astype(BF16)
    w_ffn_in_b = w_ffn_in.astype(BF16)
    w_ffn_out_b = w_ffn_out.astype(BF16)

    pad_lanes = lambda v: jnp.pad(v.reshape(depth, 1, n_dt), ((0, 0), (0, 0), (0, LANES - n_dt)))
    dt_bias_p = pad_lanes(ssm_dt_bias.astype(F32))
    a_log_p = pad_lanes(ssm_A_log.astype(F32))
    d_skip = jnp.repeat(ssm_D.astype(F32), SSM_HEAD_DIM, axis=1).reshape(depth, 1, inner)
    slopes = jnp.asarray(_alibi_slopes(DA_HEADS))
    conv_b3 = ssm_conv_b.reshape(depth, 1, conv_ch)

    kv_all = _mem_kv(mem.reshape(b * m, d), mem_norm_g.reshape(depth, 1, d), w_kv_b)

    x2 = x.reshape(t, d)
    for i in range(depth):
        lambda_init = 0.8 - 0.6 * math.exp(-0.3 * i)
        lam = (jnp.exp(jnp.sum(da_lambda_q1[i] * da_lambda_k1[i]).astype(F32))
               - jnp.exp(jnp.sum(da_lambda_q2[i] * da_lambda_k2[i]).astype(F32))
               + lambda_init).reshape(1)

        proj2, xbc2, dt_raw = _in_proj(x2, mix_norm_g[i].reshape(1, d), w_plain, w_conv, w_dt,
                                       ssm_conv_w, conv_b3, i, seq_len=s, steps=IN_PROJ_STEPS)
        proj3 = proj2.reshape(b, s, -1)
        xbc3 = xbc2.reshape(b, s, conv_ch)
        dt3 = dt_raw.reshape(b, s, LANES)

        y_bwd = _ssd(xbc3, 0, dt3, dt_bias_p[i], a_log_p[i], backward=True)
        y_ssm = _ssd(xbc3, 0, dt3, dt_bias_p[i], a_log_p[i], backward=False,
                     fused=(y_bwd, proj3, c_z, d_skip[i], ssm_norm_g[i].reshape(1, inner)))

        y_da = _diff_attn(proj3, slopes, lam, da_subln_g[i].reshape(1, 2 * DA_HEAD_DIM),
                          c_q, c_k, c_v, lambda_init)
        y_xa = _xattn(proj3, kv_all, c_xq, xw, i)

        x2 = _merge(y_da.reshape(t, -1), y_ssm.reshape(t, inner), y_xa.reshape(t, xw),
                    proj2, c_gt, x2, w_branch_b, w_out_b, i)
        x2 = _ffn(x2, ffn_norm_g[i].reshape(1, d), w_ffn_in_b, w_ffn_out_b,
                  final_norm_g.reshape(1, d), final_norm=(i == depth - 1), layer=i)
    return x2.reshape(b, s, d)
```
